```python
import math
import jax, jax.numpy as jnp
from jax import lax
import numpy as np

D_MODEL = 1024
BATCH = 2
SEQ = 8192
DEPTH = 2

N_MIXERS = 2
N_POOL_LAYERS = (DEPTH + 1) // 2
N_GDN_LAYERS = DEPTH // 2
PLE_DIM = 256
D_FF = 4 * D_MODEL
POOL_WINDOWS = (2, 4, 8, 16)
N_POOL_GROUPS = len(POOL_WINDOWS)
POOL_GROUP = D_MODEL // N_POOL_GROUPS
GDN_HEADS = 8
GDN_HEAD_DIM = 128
GDN_KEY_DIM = GDN_HEADS * GDN_HEAD_DIM
GDN_VAL_DIM = GDN_HEADS * GDN_HEAD_DIM
GDN_CONV_DIM = 2 * GDN_KEY_DIM + GDN_VAL_DIM
GDN_IN_DIM = GDN_CONV_DIM + GDN_VAL_DIM + 2 * GDN_HEADS
CONV_WIDTH = 4
CHUNK = 64
DEEPNORM_ALPHA = (2.0 * DEPTH) ** 0.25
DEEPNORM_BETA = (8.0 * DEPTH) ** -0.25
LN_EPS = 1e-5
RMS_EPS = 1e-6
L2_EPS = 1e-6

kernel_name = "pool_gdn_deepnorm_hybrid"


def layer_norm(x, g, b):
    xf = x.astype(jnp.float32)
    mu = jnp.mean(xf, axis=-1, keepdims=True)
    var = jnp.mean(jnp.square(xf - mu), axis=-1, keepdims=True)
    return ((xf - mu) * lax.rsqrt(var + LN_EPS) * g + b).astype(x.dtype)


def pool_mixer(x, w_grp, b_grp, scale):
    B, S, _ = x.shape
    xg = x.astype(jnp.float32).reshape(B, S, N_POOL_GROUPS, POOL_GROUP)
    cs = jnp.cumsum(xg, axis=1)
    pos = jnp.arange(1, S + 1, dtype=jnp.float32)[None, :, None]
    outs = []
    for gi, w in enumerate(POOL_WINDOWS):
        c = cs[:, :, gi]
        c_prev = jnp.pad(c, ((0, 0), (w, 0), (0, 0)))[:, :S]
        mean = (c - c_prev) / jnp.minimum(pos, float(w))
        outs.append(mean - xg[:, :, gi])
    pooled = jnp.stack(outs, axis=2).astype(x.dtype)
    y = jnp.einsum('bsgc,gcd->bsgd', pooled, w_grp) + b_grp
    return y.reshape(B, S, D_MODEL) * scale


def causal_depthwise_conv(x, w):
    C = x.shape[-1]
    return lax.conv_general_dilated(
        x, w[:, None, :].astype(x.dtype), window_strides=(1,),
        padding=[(CONV_WIDTH - 1, 0)], dimension_numbers=('NWC', 'WIO', 'NWC'),
        feature_group_count=C)


def l2_normalize(t):
    return t * lax.rsqrt(jnp.sum(jnp.square(t), axis=-1, keepdims=True) + L2_EPS)


def chunk_gated_delta_rule(q, k, v, g, beta):
    B, S, H, Dk = q.shape
    Dv = v.shape[-1]
    N = S // CHUNK

    def to_chunks(t):
        return t.reshape(B, N, CHUNK, H, -1).transpose(0, 3, 1, 2, 4)

    q, k, v = to_chunks(q), to_chunks(k), to_chunks(v)
    g = g.reshape(B, N, CHUNK, H).transpose(0, 3, 1, 2)
    beta = beta.reshape(B, N, CHUNK, H).transpose(0, 3, 1, 2)
    g = jnp.cumsum(g, axis=-1)

    idx = jnp.arange(CHUNK)
    causal = idx[:, None] >= idx[None, :]
    strict = idx[:, None] > idx[None, :]
    decay = jnp.exp(jnp.where(causal, g[..., :, None] - g[..., None, :], -jnp.inf))

    k_beta = k * beta[..., None]
    a = jnp.einsum('bhncd,bhnmd->bhncm', k_beta, k) * decay
    m = jnp.where(strict, a, 0.0) + jnp.eye(CHUNK, dtype=jnp.float32)
    rhs = jnp.concatenate([v * beta[..., None], k_beta * jnp.exp(g)[..., None]], axis=-1)
    sol = lax.linalg.triangular_solve(m, rhs, left_side=True, lower=True, unit_diagonal=True)
    u, w = sol[..., :Dv], sol[..., Dv:]

    qk = jnp.einsum('bhncd,bhnmd->bhncm', q, k) * decay
    q_dec = q * jnp.exp(g)[..., None]
    k_dec = k * jnp.exp(g[..., -1:] - g)[..., None]
    g_last = jnp.exp(g[..., -1])

    def step(state, xs):
        u_c, w_c, qk_c, qd_c, kd_c, gl_c = xs
        v_new = u_c - jnp.einsum('bhcd,bhde->bhce', w_c, state)
        o_c = (jnp.einsum('bhcd,bhde->bhce', qd_c, state)
               + jnp.einsum('bhcm,bhme->bhce', qk_c, v_new))
        state = state * gl_c[..., None, None] + jnp.einsum('bhcd,bhce->bhde', kd_c, v_new)
        return state, o_c

    xs = tuple(jnp.moveaxis(t, 2, 0) for t in (u, w, qk, q_dec, k_dec, g_last))
    s0 = jnp.zeros((B, H, Dk, Dv), jnp.float32)
    _, o = lax.scan(step, s0, xs)
    return o.transpose(1, 0, 3, 2, 4).reshape(B, S, H, Dv)


def gated_deltanet(x, w_in, conv_w, a_log, dt_bias, norm_w, w_out):
    B, S, _ = x.shape
    H, Dh = GDN_HEADS, GDN_HEAD_DIM
    proj = x @ w_in
    qkv, z, b_logit, a_logit = jnp.split(
        proj, [GDN_CONV_DIM, GDN_CONV_DIM + GDN_VAL_DIM, GDN_CONV_DIM + GDN_VAL_DIM + H], axis=-1)
    qkv = jax.nn.silu(causal_depthwise_conv(qkv, conv_w))
    q, k, v = jnp.split(qkv, [GDN_KEY_DIM, 2 * GDN_KEY_DIM], axis=-1)
    q = l2_normalize(q.reshape(B, S, H, Dh).astype(jnp.float32)) * (Dh ** -0.5)
    k = l2_normalize(k.reshape(B, S, H, Dh).astype(jnp.float32))
    v = v.reshape(B, S, H, Dh).astype(jnp.float32)
    beta = jax.nn.sigmoid(b_logit.astype(jnp.float32))
    g = -jnp.exp(a_log.astype(jnp.float32)) * jax.nn.softplus(
        a_logit.astype(jnp.float32) + dt_bias.astype(jnp.float32))
    o = chunk_gated_delta_rule(q, k, v, g, beta)
    zf = z.reshape(B, S, H, Dh).astype(jnp.float32)
    o = o * lax.rsqrt(jnp.mean(jnp.square(o), axis=-1, keepdims=True) + RMS_EPS) * norm_w * jax.nn.silu(zf)
    return o.reshape(B, S, GDN_VAL_DIM).astype(x.dtype) @ w_out


def squared_relu_mlp(x, w1, w2):
    return jnp.square(jax.nn.relu(x @ w1)) @ w2


def setup_inputs(seed: int = 0) -> dict:
    key = jax.random.key(seed)
    ks = jax.random.split(key, 20)
    f32 = jnp.float32
    nrm = lambda k, s: jax.random.normal(k, s, f32)
    dt = jnp.exp(jax.random.uniform(ks[9], (N_GDN_LAYERS, GDN_HEADS), f32,
                                    math.log(1e-3), math.log(1e-1)))
    return {
        "x": nrm(ks[0], (BATCH, SEQ, D_MODEL)),
        "p": nrm(ks[1], (DEPTH, BATCH, SEQ, PLE_DIM)),
        "ln_gain": 1.0 + 0.02 * nrm(ks[2], (DEPTH, 2, D_MODEL)),
        "ln_bias": 0.02 * nrm(ks[3], (DEPTH, 2, D_MODEL)),
        "pool_w": nrm(ks[4], (N_POOL_LAYERS, N_POOL_GROUPS, POOL_GROUP, POOL_GROUP)) * (POOL_GROUP ** -0.5) * DEEPNORM_BETA,
        "pool_b": 0.02 * nrm(ks[5], (N_POOL_LAYERS, N_POOL_GROUPS, POOL_GROUP)),
        "pool_scale": 1.0 + 0.1 * nrm(ks[6], (N_POOL_LAYERS, D_MODEL)),
        "gdn_w_in": nrm(ks[7], (N_GDN_LAYERS, D_MODEL, GDN_IN_DIM)) * (D_MODEL ** -0.5),
        "gdn_conv": nrm(ks[8], (N_GDN_LAYERS, CONV_WIDTH, GDN_CONV_DIM)) * (CONV_WIDTH ** -0.5),
        "gdn_a_log": jnp.log(jax.random.uniform(ks[10], (N_GDN_LAYERS, GDN_HEADS), f32, 1.0, 16.0)),
        "gdn_dt_bias": dt + jnp.log(-jnp.expm1(-dt)),
        "gdn_norm_w": 1.0 + 0.02 * nrm(ks[11], (N_GDN_LAYERS, GDN_HEAD_DIM)),
        "gdn_w_out": nrm(ks[12], (N_GDN_LAYERS, GDN_VAL_DIM, D_MODEL)) * (GDN_VAL_DIM ** -0.5) * DEEPNORM_BETA,
        "mlp_w1": nrm(ks[13], (DEPTH, D_MODEL, D_FF)) * (D_MODEL ** -0.5),
        "mlp_w2": nrm(ks[14], (DEPTH, D_FF, D_MODEL)) * (D_FF ** -0.5) * DEEPNORM_BETA,
        "ple_gate_w": nrm(ks[15], (DEPTH, D_MODEL, D_MODEL)) * (D_MODEL ** -0.5),
        "ple_gate_b": 0.02 * nrm(ks[16], (DEPTH, D_MODEL)),
        "ple_proj": nrm(ks[17], (DEPTH, PLE_DIM, D_MODEL)) * (PLE_DIM ** -0.5),
    }


def reference(x, p, ln_gain, ln_bias, pool_w, pool_b, pool_scale, gdn_w_in, gdn_conv,
              gdn_a_log, gdn_dt_bias, gdn_norm_w, gdn_w_out, mlp_w1, mlp_w2,
              ple_gate_w, ple_gate_b, ple_proj):
    for i in range(DEPTH):
        j = i // N_MIXERS
        if i % N_MIXERS == 0:
            mix = pool_mixer(x, pool_w[j], pool_b[j], pool_scale[j])
        else:
            mix = gated_deltanet(x, gdn_w_in[j], gdn_conv[j], gdn_a_log[j], gdn_dt_bias[j],
                                 gdn_norm_w[j], gdn_w_out[j])
        x = layer_norm(DEEPNORM_ALPHA * x + mix, ln_gain[i, 0], ln_bias[i, 0])
        ff = squared_relu_mlp(x, mlp_w1[i], mlp_w2[i])
        pe = jax.nn.sigmoid(x @ ple_gate_w[i] + ple_gate_b[i]) * (p[i] @ ple_proj[i])
        x = layer_norm(DEEPNORM_ALPHA * x + ff + pe, ln_gain[i, 1], ln_bias[i, 1])
    return x
```

```python
import functools

import jax
import jax.numpy as jnp
from jax import lax
from jax.experimental import pallas as pl
from jax.experimental.pallas import tpu as pltpu

F32 = jnp.float32
BF16 = jnp.bfloat16

DEPTH = 2
POOL_WINDOWS = (2, 4, 8, 16)
POOL_HALO = 16
GDN_HEADS = 8
HEAD_DIM = 128
CONV_WIDTH = 4
CONV_HALO = 8
CHUNK = 64
DEEPNORM_ALPHA = (2.0 * DEPTH) ** 0.25
LN_EPS = 1e-5
RMS_EPS = 1e-6
L2_EPS = 1e-6

LANES = 128
ROW_TILE = 512
DELTA_TILE = 256
FF_CHUNK = 1024
VMEM_LIMIT = 56 * 1024 * 1024


def _dot(a, b):
    return jnp.dot(a, b, preferred_element_type=F32)


def _dot_nt(a, b):
    return lax.dot_general(a, b, (((1,), (1,)), ((), ())), preferred_element_type=F32)


def _split_bf16(x):
    hi = x.astype(BF16)
    lo = (x - hi.astype(F32)).astype(BF16)
    return hi, lo


def _dot_split(x, w_bf16):
    hi, lo = _split_bf16(x)
    return _dot(hi, w_bf16) + _dot(lo, w_bf16)


def _layer_norm(y, g, b):
    mu = jnp.mean(y, axis=-1, keepdims=True)
    d = y - mu
    var = jnp.mean(d * d, axis=-1, keepdims=True)
    return d * lax.rsqrt(var + LN_EPS) * g + b


def _layer_tail(y, p_tile, ln_ref, w1_ref, w2_ref, gw_ref, gb_ref, pp_ref):
    x1 = _layer_norm(y, ln_ref[0:1, :], ln_ref[1:2, :])
    x1b = x1.astype(BF16)
    d_ff = w1_ref.shape[1]
    acc = None
    for c in range(d_ff // FF_CHUNK):
        cs = slice(c * FF_CHUNK, (c + 1) * FF_CHUNK)
        h = _dot(x1b, w1_ref[:, cs])
        h = jnp.square(jnp.maximum(h, 0.0)).astype(BF16)
        t = _dot(h, w2_ref[cs, :])
        acc = t if acc is None else acc + t
    gate = jax.nn.sigmoid(_dot(x1b, gw_ref[...]) + gb_ref[...])
    pe = gate * _dot(p_tile.astype(BF16), pp_ref[...])
    y2 = DEEPNORM_ALPHA * x1 + acc + pe
    return _layer_norm(y2, ln_ref[2:3, :], ln_ref[3:4, :])


def _pool_layer_kernel(x_ref, halo_ref, p_ref, pw_ref, pb_ref, ps_ref, ln_ref, w1_ref, w2_ref,
                       gw_ref, gb_ref, pp_ref, o_ref, xe_ref):
    i = pl.program_id(1)
    tm = x_ref.shape[0]
    group = pw_ref.shape[1]
    x = x_ref[...]
    xe_ref[0:POOL_HALO, :] = jnp.where(i == 0, 0.0, halo_ref[...])
    xe_ref[POOL_HALO:, :] = x
    pos = (lax.broadcasted_iota(jnp.int32, (tm, 1), 0) + (i * tm + 1)).astype(F32)
    mixes = []
    for gi, w in enumerate(POOL_WINDOWS):
        sl = slice(gi * group, (gi + 1) * group)
        xg = x[:, sl]
        s = xg
        for j in range(1, w):
            s = s + xe_ref[POOL_HALO - j:POOL_HALO - j + tm, sl]
        mean = s * (1.0 / jnp.minimum(pos, float(w)))
        pooled = (mean - xg).astype(BF16)
        mixes.append(_dot(pooled, pw_ref[gi]) + pb_ref[gi:gi + 1, :])
    mix = jnp.concatenate(mixes, axis=-1) * ps_ref[...]
    y = DEEPNORM_ALPHA * x + mix
    o_ref[...] = _layer_tail(y, p_ref[...], ln_ref, w1_ref, w2_ref, gw_ref, gb_ref, pp_ref)


def _const_spec(shape):
    nd = len(shape)
    return pl.BlockSpec(shape, lambda b, i: (0,) * nd, pipeline_mode=pl.Buffered(1))


def _tail_specs(d, d_ff, ple):
    return [_const_spec((4, d)), _const_spec((d, d_ff)), _const_spec((d_ff, d)),
            _const_spec((d, d)), _const_spec((1, d)), _const_spec((ple, d))]


def _pool_layer(x, p, layer, pw, pb, ps, tail_w):
    B, S, D = x.shape
    ple = p.shape[-1]
    tm = ROW_TILE
    halo_blocks = tm // POOL_HALO
    row_spec = pl.BlockSpec((None, tm, D), lambda b, i: (b, i, 0))
    in_specs = [
        row_spec,
        pl.BlockSpec((None, POOL_HALO, D), lambda b, i: (b, jnp.maximum(i * halo_blocks - 1, 0), 0)),
        pl.BlockSpec((None, None, tm, ple), lambda b, i: (layer, b, i, 0)),
        _const_spec(pw.shape), _const_spec(pb.shape), _const_spec(ps.shape),
    ] + _tail_specs(D, tail_w[1].shape[1], ple)
    return pl.pallas_call(
        _pool_layer_kernel,
        grid=(B, S // tm),
        in_specs=in_specs,
        out_specs=row_spec,
        out_shape=jax.ShapeDtypeStruct((B, S, D), F32),
        scratch_shapes=[pltpu.VMEM((tm + POOL_HALO, D), F32)],
        compiler_params=pltpu.CompilerParams(
            dimension_semantics=("arbitrary", "arbitrary"), vmem_limit_bytes=VMEM_LIMIT),
        name="pool_layer",
    )(x, x, p, pw, pb, ps, *tail_w)


def _softplus(x):
    return jnp.maximum(x, 0.0) + jnp.log1p(jnp.exp(-jnp.abs(x)))


def _gdn_proj_kernel(x_ref, wqkv_ref, wz_ref, wba_ref, cw_ref, alog_ref, dtb_ref, tri_ref, eb_ref, ea_ref,
                     q_ref, k_ref, v_ref, z_ref, beta_ref, gc_ref, carry_ref, ext_ref):
    i = pl.program_id(1)
    tm, d = x_ref.shape

    @pl.when(i == 0)
    def _():
        carry_ref[...] = jnp.zeros_like(carry_ref)

    xb = x_ref[...].astype(BF16)
    for part, out_ref in enumerate((q_ref, k_ref, v_ref)):
        cs = slice(part * d, (part + 1) * d)
        pre = _dot(xb, wqkv_ref[:, cs])
        ext_ref[0:CONV_HALO, :] = carry_ref[part]
        ext_ref[CONV_HALO:, :] = pre
        carry_ref[part] = pre[tm - CONV_HALO:, :]
        conv = pre * cw_ref[CONV_WIDTH - 1:CONV_WIDTH, cs]
        for j in range(CONV_WIDTH - 1):
            off = CONV_HALO - (CONV_WIDTH - 1) + j
            conv = conv + ext_ref[off:off + tm, :] * cw_ref[j:j + 1, cs]
        act = conv * jax.nn.sigmoid(conv)
        if part == 2:
            out_ref[...] = act
        else:
            scale = HEAD_DIM ** -0.5 if part == 0 else 1.0
            for h in range(GDN_HEADS):
                hs = slice(h * HEAD_DIM, (h + 1) * HEAD_DIM)
                a = act[:, hs]
                r = lax.rsqrt(jnp.sum(a * a, axis=-1, keepdims=True) + L2_EPS)
                out_ref[:, hs] = a * (r * scale)

    z_ref[...] = _dot(xb, wz_ref[...])
    ba = _dot(xb, wba_ref[...])
    beta = jax.nn.sigmoid(ba)
    g = -jnp.exp(alog_ref[...]) * _softplus(ba + dtb_ref[...])
    g_hi, g_lo = _split_bf16(g)
    gc = _dot(tri_ref[...], g_hi) + _dot(tri_ref[...], g_lo)
    beta_ref[...] = _dot_split(beta, eb_ref[...])
    gc_ref[...] = _dot_split(gc, ea_ref[...])


def _gdn_proj(x, wqkv, wz, wba, cw, alog, dtb):
    B, S, D = x.shape
    tm = ROW_TILE
    r = jnp.arange(tm)
    tri = ((r[:, None] >= r[None, :]) & (r[:, None] // CHUNK == r[None, :] // CHUNK)).astype(BF16)
    lane_head = jnp.arange(GDN_HEADS * HEAD_DIM) // HEAD_DIM
    src = jnp.arange(LANES)
    eb = (src[:, None] == lane_head[None, :]).astype(BF16)
    ea = (src[:, None] == lane_head[None, :] + GDN_HEADS).astype(BF16)
    row_spec = pl.BlockSpec((None, tm, D), lambda b, i: (b, i, 0))
    consts = (wqkv, wz, wba, cw, alog, dtb, tri, eb, ea)
    return pl.pallas_call(
        _gdn_proj_kernel,
        grid=(B, S // tm),
        in_specs=[row_spec] + [_const_spec(c.shape) for c in consts],
        out_specs=[row_spec] * 6,
        out_shape=[jax.ShapeDtypeStruct((B, S, D), F32)] * 6,
        scratch_shapes=[pltpu.VMEM((3, CONV_HALO, D), F32), pltpu.VMEM((tm + CONV_HALO, D), F32)],
        compiler_params=pltpu.CompilerParams(
            dimension_semantics=("arbitrary", "arbitrary"), vmem_limit_bytes=VMEM_LIMIT),
        name="gdn_proj",
    )(x, *consts)


def _unit_lower_inverse(n):
    c = n.shape[0]
    ii = lax.broadcasted_iota(jnp.int32, (c, c), 0)
    jj = lax.broadcasted_iota(jnp.int32, (c, c), 1)
    t = jnp.where(ii == jj, 1.0, 0.0) - n
    pw = _dot(n.astype(BF16), n.astype(BF16))
    terms = 2
    while terms < c:
        pwb = pw.astype(BF16)
        if 2 * terms < c:
            both = _dot(jnp.concatenate([pw, t], axis=0).astype(BF16), pwb)
            pw, t = both[:c], t + both[c:]
        else:
            t = t + _dot(t.astype(BF16), pwb)
        terms *= 2
    return t


def _gdn_delta_kernel(q_ref, k_ref, v_ref, beta_ref, gc_ref, o_ref, state_ref):
    i = pl.program_id(1)
    ts = q_ref.shape[0]
    c = CHUNK

    @pl.when(i == 0)
    def _():
        state_ref[...] = jnp.zeros_like(state_ref)

    ii = lax.broadcasted_iota(jnp.int32, (c, c), 0)
    jj = lax.broadcasted_iota(jnp.int32, (c, c), 1)
    causal = ii >= jj
    strict = ii > jj
    first_lane = lax.broadcasted_iota(jnp.int32, (c, HEAD_DIM), 1) == 0
    ones = jnp.ones((c, HEAD_DIM), BF16)

    def chunk_body(ci, carry):
        rows = pl.ds(pl.multiple_of(ci * c, c), c)
        for h in range(GDN_HEADS):
            hs = slice(h * HEAD_DIM, (h + 1) * HEAD_DIM)
            q = q_ref[rows, hs]
            k = k_ref[rows, hs]
            v = v_ref[rows, hs]
            beta = beta_ref[rows, hs]
            gc = gc_ref[rows, hs]
            g_last = gc[c - 1:c, :]
            eg = jnp.exp(gc)
            kb = k * beta
            gm_hi, gm_lo = _split_bf16(jnp.where(first_lane, gc, 0.0))
            gj = _dot_nt(ones, gm_hi) + _dot_nt(ones, gm_lo)
            decay = jnp.exp(jnp.where(causal, gc[:, :c] - gj, -jnp.inf))
            akq = _dot_nt(jnp.concatenate([kb, q], axis=0).astype(BF16), k.astype(BF16))
            n = jnp.where(strict, akq[:c] * decay, 0.0)
            qk = akq[c:] * decay
            t_inv = _unit_lower_inverse(n)
            rhs = jnp.concatenate([v * beta, kb * eg], axis=1).astype(BF16)
            uw = _dot(t_inv.astype(BF16), rhs).astype(BF16)
            kd_t = (k * jnp.exp(g_last - gc)).T
            f = _dot(jnp.concatenate([qk, kd_t], axis=0).astype(BF16), uw)
            m = jnp.concatenate([-f[c:, HEAD_DIM:], q * eg - f[:c, HEAD_DIM:]], axis=0).astype(BF16)
            state = state_ref[h]
            r = _dot(m, state.astype(BF16))
            state_ref[h] = jnp.exp(g_last) * state + r[:HEAD_DIM] + f[c:, :HEAD_DIM]
            o_ref[rows, hs] = r[HEAD_DIM:] + f[:c, :HEAD_DIM]
        return carry

    lax.fori_loop(0, ts // c, chunk_body, 0)


def _gdn_delta(q, k, v, beta, gc):
    B, S, D = q.shape
    ts = DELTA_TILE
    row_spec = pl.BlockSpec((None, ts, D), lambda b, i: (b, i, 0))
    return pl.pallas_call(
        _gdn_delta_kernel,
        grid=(B, S // ts),
        in_specs=[row_spec] * 5,
        out_specs=row_spec,
        out_shape=jax.ShapeDtypeStruct((B, S, D), F32),
        scratch_shapes=[pltpu.VMEM((GDN_HEADS, HEAD_DIM, HEAD_DIM), F32)],
        compiler_params=pltpu.CompilerParams(
            dimension_semantics=("arbitrary", "arbitrary"), vmem_limit_bytes=VMEM_LIMIT),
        name="gdn_delta",
    )(q, k, v, beta, gc)


def _gdn_layer_kernel(o_ref, z_ref, x_ref, p_ref, nw_ref, wo_ref, ln_ref, w1_ref, w2_ref,
                      gw_ref, gb_ref, pp_ref, out_ref):
    o = o_ref[...]
    z = z_ref[...]
    normed = []
    for h in range(GDN_HEADS):
        hs = slice(h * HEAD_DIM, (h + 1) * HEAD_DIM)
        oh = o[:, hs]
        normed.append(oh * lax.rsqrt(jnp.mean(oh * oh, axis=-1, keepdims=True) + RMS_EPS))
    gated = jnp.concatenate(normed, axis=-1) * nw_ref[...] * (z * jax.nn.sigmoid(z))
    mix = _dot(gated.astype(BF16), wo_ref[...])
    y = DEEPNORM_ALPHA * x_ref[...] + mix
    out_ref[...] = _layer_tail(y, p_ref[...], ln_ref, w1_ref, w2_ref, gw_ref, gb_ref, pp_ref)


def _gdn_layer(o, z, x, p, layer, nw, wo, tail_w):
    B, S, D = x.shape
    ple = p.shape[-1]
    tm = ROW_TILE
    row_spec = pl.BlockSpec((None, tm, D), lambda b, i: (b, i, 0))
    in_specs = [
        row_spec, row_spec, row_spec,
        pl.BlockSpec((None, None, tm, ple), lambda b, i: (layer, b, i, 0)),
        _const_spec(nw.shape), _const_spec(wo.shape),
    ] + _tail_specs(D, tail_w[1].shape[1], ple)
    return pl.pallas_call(
        _gdn_layer_kernel,
        grid=(B, S // tm),
        in_specs=in_specs,
        out_specs=row_spec,
        out_shape=jax.ShapeDtypeStruct((B, S, D), F32),
        compiler_params=pltpu.CompilerParams(
            dimension_semantics=("arbitrary", "arbitrary"), vmem_limit_bytes=VMEM_LIMIT),
        name="gdn_layer",
    )(o, z, x, p, nw, wo, *tail_w)


def _tail_weights(i, ln_gain, ln_bias, mlp_w1, mlp_w2, ple_gate_w, ple_gate_b, ple_proj):
    ln = jnp.stack([ln_gain[i, 0], ln_bias[i, 0], ln_gain[i, 1], ln_bias[i, 1]])
    return (ln, mlp_w1[i].astype(BF16), mlp_w2[i].astype(BF16), ple_gate_w[i].astype(BF16),
            ple_gate_b[i][None, :], ple_proj[i].astype(BF16))


def _pad_lanes(vec, offset):
    return jnp.zeros((1, LANES), F32).at[0, offset:offset + vec.shape[0]].set(vec)


def kernel(x, p, ln_gain, ln_bias, pool_w, pool_b, pool_scale, gdn_w_in, gdn_conv, gdn_a_log, gdn_dt_bias,
           gdn_norm_w, gdn_w_out, mlp_w1, mlp_w2, ple_gate_w, ple_gate_b, ple_proj):
    B, S, D = x.shape
    assert S % ROW_TILE == 0 and S % DELTA_TILE == 0 and ROW_TILE % CHUNK == 0 and DELTA_TILE % CHUNK == 0
    assert D == GDN_HEADS * HEAD_DIM and D % len(POOL_WINDOWS) == 0
    tail_args = (ln_gain, ln_bias, mlp_w1, mlp_w2, ple_gate_w, ple_gate_b, ple_proj)

    x = _pool_layer(x, p, 0, pool_w[0].astype(BF16), pool_b[0], pool_scale[0][None, :],
                    _tail_weights(0, *tail_args))

    w_in = gdn_w_in[0]
    conv_dim = 3 * D
    wqkv = w_in[:, :conv_dim].astype(BF16)
    wz = w_in[:, conv_dim:conv_dim + D].astype(BF16)
    wba = jnp.pad(w_in[:, conv_dim + D:], ((0, 0), (0, LANES - 2 * GDN_HEADS))).astype(BF16)
    alog = _pad_lanes(gdn_a_log[0], GDN_HEADS)
    dtb = _pad_lanes(gdn_dt_bias[0], GDN_HEADS)
    q, k, v, z, beta, gc = _gdn_proj(x, wqkv, wz, wba, gdn_conv[0], alog, dtb)
    o = _gdn_delta(q, k, v, beta, gc)
    nw = jnp.tile(gdn_norm_w[0], GDN_HEADS)[None, :]
    return _gdn_layer(o, z, x, p, 1, nw, gdn_w_out[0].astype(BF16), _tail_weights(1, *tail_args))
```

```python
import functools

import jax
import jax.numpy as jnp
from jax import lax
from jax.experimental import pallas as pl
from jax.experimental.pallas import tpu as pltpu

F32 = jnp.float32
BF16 = jnp.bfloat16

DEPTH = 2
POOL_WINDOWS = (2, 4, 8, 16)
POOL_HALO = 16
GDN_HEADS = 8
HEAD_DIM = 128
CONV_WIDTH = 4
CONV_HALO = 8
CHUNK = 64
DEEPNORM_ALPHA = (2.0 * DEPTH) ** 0.25
LN_EPS = 1e-5
RMS_EPS = 1e-6
L2_EPS = 1e-6

LANES = 128
ROW_TILE = 512
DELTA_TILE = 256
FF_CHUNK = 1024
VMEM_LIMIT = 56 * 1024 * 1024


def _dot(a, b):
    return jnp.dot(a, b, preferred_element_type=F32)


def _dot_nt(a, b):
    return lax.dot_general(a, b, (((1,), (1,)), ((), ())), preferred_element_type=F32)


def _split_bf16(x):
    hi = x.astype(BF16)
    lo = (x - hi.astype(F32)).astype(BF16)
    return hi, lo


def _dot_split(x, w_bf16):
    hi, lo = _split_bf16(x)
    return _dot(hi, w_bf16) + _dot(lo, w_bf16)


def _layer_norm(y, g, b):
    mu = jnp.mean(y, axis=-1, keepdims=True)
    d = y - mu
    var = jnp.mean(d * d, axis=-1, keepdims=True)
    return d * lax.rsqrt(var + LN_EPS) * g + b


def _layer_tail(y, p_tile, ln_ref, w1_ref, w2_ref, gw_ref, gb_ref, pp_ref):
    x1 = _layer_norm(y, ln_ref[0:1, :], ln_ref[1:2, :])
    x1b = x1.astype(BF16)
    d_ff = w1_ref.shape[1]
    acc = None
    for c in range(d_ff // FF_CHUNK):
        cs = slice(c * FF_CHUNK, (c + 1) * FF_CHUNK)
        h = _dot(x1b, w1_ref[:, cs])
        h = jnp.square(jnp.maximum(h, 0.0)).astype(BF16)
        t = _dot(h, w2_ref[cs, :])
        acc = t if acc is None else acc + t
    gate = jax.nn.sigmoid(_dot(x1b, gw_ref[...]) + gb_ref[...])
    pe = gate * _dot(p_tile.astype(BF16), pp_ref[...])
    y2 = DEEPNORM_ALPHA * x1 + acc + pe
    return _layer_norm(y2, ln_ref[2:3, :], ln_ref[3:4, :])


def _pool_layer_kernel(x_ref, halo_ref, p_ref, pw_ref, pb_ref, ps_ref, ln_ref, w1_ref, w2_ref,
                       gw_ref, gb_ref, pp_ref, o_ref, xe_ref):
    i = pl.program_id(1)
    tm = x_ref.shape[0]
    group = pw_ref.shape[1]
    x = x_ref[...]
    xe_ref[0:POOL_HALO, :] = jnp.where(i == 0, 0.0, halo_ref[...])
    xe_ref[POOL_HALO:, :] = x
    pos = (lax.broadcasted_iota(jnp.int32, (tm, 1), 0) + (i * tm + 1)).astype(F32)
    mixes = []
    for gi, w in enumerate(POOL_WINDOWS):
        sl = slice(gi * group, (gi + 1) * group)
        xg = x[:, sl]
        s = xg
        for j in range(1, w):
            s = s + xe_ref[POOL_HALO - j:POOL_HALO - j + tm, sl]
        mean = s * (1.0 / jnp.minimum(pos, float(w)))
        pooled = (mean - xg).astype(BF16)
        mixes.append(_dot(pooled, pw_ref[gi]) + pb_ref[gi:gi + 1, :])
    mix = jnp.concatenate(mixes, axis=-1) * ps_ref[...]
    y = DEEPNORM_ALPHA * x + mix
    o_ref[...] = _layer_tail(y, p_ref[...], ln_ref, w1_ref, w2_ref, gw_ref, gb_ref, pp_ref)


def _const_spec(shape):
    nd = len(shape)
    return pl.BlockSpec(shape, lambda b, i: (0,) * nd, pipeline_mode=pl.Buffered(1))


def _tail_specs(d, d_ff, ple):
    return [_const_spec((4, d)), _const_spec((d, d_ff)), _const_spec((d_ff, d)),
            _const_spec((d, d)), _const_spec((1, d)), _const_spec((ple, d))]


def _pool_layer(x, p, layer, pw, pb, ps, tail_w):
    B, S, D = x.shape
    ple = p.shape[-1]
    tm = ROW_TILE
    halo_blocks = tm // POOL_HALO
    row_spec = pl.BlockSpec((None, tm, D), lambda b, i: (b, i, 0))
    in_specs = [
        row_spec,
        pl.BlockSpec((None, POOL_HALO, D), lambda b, i: (b, jnp.maximum(i * halo_blocks - 1, 0), 0)),
        pl.BlockSpec((None, None, tm, ple), lambda b, i: (layer, b, i, 0)),
        _const_spec(pw.shape), _const_spec(pb.shape), _const_spec(ps.shape),
    ] + _tail_specs(D, tail_w[1].shape[1], ple)
    return pl.pallas_call(
        _pool_layer_kernel,
        grid=(B, S // tm),
        in_specs=in_specs,
        out_specs=row_spec,
        out_shape=jax.ShapeDtypeStruct((B, S, D), F32),
        scratch_shapes=[pltpu.VMEM((tm + POOL_HALO, D), F32)],
        compiler_params=pltpu.CompilerParams(
            dimension_semantics=("arbitrary", "arbitrary"), vmem_limit_bytes=VMEM_LIMIT),
        name="pool_layer",
    )(x, x, p, pw, pb, ps, *tail_w)


def _softplus(x):
    return jnp.maximum(x, 0.0) + jnp.log1p(jnp.exp(-jnp.abs(x)))


def _gdn_proj_kernel(x_ref, wqkv_ref, wz_ref, wba_ref, cw_ref, alog_ref, dtb_ref, tri_ref, eb_ref, ea_ref,
                     q_ref, k_ref, v_ref, z_ref, beta_ref, gc_ref, carry_ref, ext_ref):
    i = pl.program_id(1)
    tm, d = x_ref.shape

    @pl.when(i == 0)
    def _():
        carry_ref[...] = jnp.zeros_like(carry_ref)

    xb = x_ref[...].astype(BF16)
    for part, out_ref in enumerate((q_ref, k_ref, v_ref)):
        cs = slice(part * d, (part + 1) * d)
        pre = _dot(xb, wqkv_ref[:, cs])
        ext_ref[0:CONV_HALO, :] = carry_ref[part]
        ext_ref[CONV_HALO:, :] = pre
        carry_ref[part] = pre[tm - CONV_HALO:, :]
        conv = pre * cw_ref[CONV_WIDTH - 1:CONV_WIDTH, cs]
        for j in range(CONV_WIDTH - 1):
            off = CONV_HALO - (CONV_WIDTH - 1) + j
            conv = conv + ext_ref[off:off + tm, :] * cw_ref[j:j + 1, cs]
        act = conv * jax.nn.sigmoid(conv)
        if part == 2:
            out_ref[...] = act
        else:
            scale = HEAD_DIM ** -0.5 if part == 0 else 1.0
            for h in range(GDN_HEADS):
                hs = slice(h * HEAD_DIM, (h + 1) * HEAD_DIM)
                a = act[:, hs]
                r = lax.rsqrt(jnp.sum(a * a, axis=-1, keepdims=True) + L2_EPS)
                out_ref[:, hs] = a * (r * scale)

    z_ref[...] = _dot(xb, wz_ref[...])
    ba = _dot(xb, wba_ref[...])
    beta = jax.nn.sigmoid(ba)
    g = -jnp.exp(alog_ref[...]) * _softplus(ba + dtb_ref[...])
    g_hi, g_lo = _split_bf16(g)
    gc = _dot(tri_ref[...], g_hi) + _dot(tri_ref[...], g_lo)
    beta_ref[...] = _dot_split(beta, eb_ref[...])
    gc_ref[...] = _dot_split(gc, ea_ref[...])


def _gdn_proj(x, wqkv, wz, wba, cw, alog, dtb):
    B, S, D = x.shape
    tm = ROW_TILE
    r = jnp.arange(tm)
    tri = ((r[:, None] >= r[None, :]) & (r[:, None] // CHUNK == r[None, :] // CHUNK)).astype(BF16)
    lane_head = jnp.arange(GDN_HEADS * HEAD_DIM) // HEAD_DIM
    src = jnp.arange(LANES)
    eb = (src[:, None] == lane_head[None, :]).astype(BF16)
    ea = (src[:, None] == lane_head[None, :] + GDN_HEADS).astype(BF16)
    row_spec = pl.BlockSpec((None, tm, D), lambda b, i: (b, i, 0))
    consts = (wqkv, wz, wba, cw, alog, dtb, tri, eb, ea)
    return pl.pallas_call(
        _gdn_proj_kernel,
        grid=(B, S // tm),
        in_specs=[row_spec] + [_const_spec(c.shape) for c in consts],
        out_specs=[row_spec] * 6,
        out_shape=[jax.ShapeDtypeStruct((B, S, D), F32)] * 6,
        scratch_shapes=[pltpu.VMEM((3, CONV_HALO, D), F32), pltpu.VMEM((tm + CONV_HALO, D), F32)],
        compiler_params=pltpu.CompilerParams(
            dimension_semantics=("arbitrary", "arbitrary"), vmem_limit_bytes=VMEM_LIMIT),
        name="gdn_proj",
    )(x, *consts)


def _gdn_delta_kernel(q_ref, k_ref, v_ref, beta_ref, gc_ref, o_ref, state_ref):
    i = pl.program_id(0)
    nb, ts, _ = q_ref.shape
    c = CHUNK
    heads = range(nb * GDN_HEADS)

    @pl.when(i == 0)
    def _():
        state_ref[...] = jnp.zeros_like(state_ref)

    ii = lax.broadcasted_iota(jnp.int32, (c, c), 0)
    jj = lax.broadcasted_iota(jnp.int32, (c, c), 1)
    causal = ii >= jj
    strict = ii > jj
    eye = jnp.where(ii == jj, 1.0, 0.0)
    first_lane = lax.broadcasted_iota(jnp.int32, (c, HEAD_DIM), 1) == 0
    ones = jnp.ones((c, HEAD_DIM), BF16)

    def chunk_body(ci, carry):
        rows = pl.ds(pl.multiple_of(ci * c, c), c)
        hs = [(h // GDN_HEADS, rows, slice(h % GDN_HEADS * HEAD_DIM, (h % GDN_HEADS + 1) * HEAD_DIM)) for h in heads]
        gc = [gc_ref[hs[h]] for h in heads]
        k = [k_ref[hs[h]] for h in heads]
        kb = [k[h] * beta_ref[hs[h]] for h in heads]
        gj = []
        for h in heads:
            gm_hi, gm_lo = _split_bf16(jnp.where(first_lane, gc[h], 0.0))
            gj.append(_dot_nt(ones, gm_hi) + _dot_nt(ones, gm_lo))
        akq = [_dot_nt(jnp.concatenate([kb[h], q_ref[hs[h]]], axis=0).astype(BF16), k[h].astype(BF16))
               for h in heads]
        decay = [jnp.exp(jnp.where(causal, gc[h][:, :c] - gj[h], -jnp.inf)) for h in heads]
        n = [jnp.where(strict, akq[h][:c] * decay[h], 0.0) for h in heads]
        qk = [akq[h][c:] * decay[h] for h in heads]

        t = [eye - n[h] for h in heads]
        pw = [_dot(n[h].astype(BF16), n[h].astype(BF16)) for h in heads]
        terms = 2
        while terms < c:
            if 2 * terms < c:
                both = [_dot(jnp.concatenate([pw[h], t[h]], axis=0).astype(BF16), pw[h].astype(BF16)) for h in heads]
                pw = [both[h][:c] for h in heads]
                t = [t[h] + both[h][c:] for h in heads]
            else:
                t = [t[h] + _dot(t[h].astype(BF16), pw[h].astype(BF16)) for h in heads]
            terms *= 2

        eg = [jnp.exp(gc[h]) for h in heads]
        uw = []
        for h in heads:
            rhs = jnp.concatenate([v_ref[hs[h]] * beta_ref[hs[h]], kb[h] * eg[h]], axis=1)
            uw.append(_dot(t[h].astype(BF16), rhs.astype(BF16)).astype(BF16))
        f = []
        for h in heads:
            kd_t = (k[h] * jnp.exp(gc[h][c - 1:c, :] - gc[h])).T
            f.append(_dot(jnp.concatenate([qk[h], kd_t], axis=0).astype(BF16), uw[h]))
        r = []
        for h in heads:
            m = jnp.concatenate([-f[h][c:, HEAD_DIM:], q_ref[hs[h]] * eg[h] - f[h][:c, HEAD_DIM:]], axis=0)
            r.append(_dot(m.astype(BF16), state_ref[h].astype(BF16)))
        for h in heads:
            state_ref[h] = eg[h][c - 1:c, :] * state_ref[h] + r[h][:HEAD_DIM] + f[h][c:, :HEAD_DIM]
            o_ref[hs[h]] = r[h][HEAD_DIM:] + f[h][:c, :HEAD_DIM]
        return carry

    lax.fori_loop(0, ts // c, chunk_body, 0)


def _gdn_delta(q, k, v, beta, gc):
    B, S, D = q.shape
    ts = DELTA_TILE
    row_spec = pl.BlockSpec((B, ts, D), lambda i: (0, i, 0))
    return pl.pallas_call(
        _gdn_delta_kernel,
        grid=(S // ts,),
        in_specs=[row_spec] * 5,
        out_specs=row_spec,
        out_shape=jax.ShapeDtypeStruct((B, S, D), F32),
        scratch_shapes=[pltpu.VMEM((B * GDN_HEADS, HEAD_DIM, HEAD_DIM), F32)],
        compiler_params=pltpu.CompilerParams(dimension_semantics=("arbitrary",), vmem_limit_bytes=VMEM_LIMIT),
        name="gdn_delta",
    )(q, k, v, beta, gc)


def _gdn_layer_kernel(o_ref, z_ref, x_ref, p_ref, nw_ref, wo_ref, ln_ref, w1_ref, w2_ref,
                      gw_ref, gb_ref, pp_ref, out_ref):
    o = o_ref[...]
    z = z_ref[...]
    normed = []
    for h in range(GDN_HEADS):
        hs = slice(h * HEAD_DIM, (h + 1) * HEAD_DIM)
        oh = o[:, hs]
        normed.append(oh * lax.rsqrt(jnp.mean(oh * oh, axis=-1, keepdims=True) + RMS_EPS))
    gated = jnp.concatenate(normed, axis=-1) * nw_ref[...] * (z * jax.nn.sigmoid(z))
    mix = _dot(gated.astype(BF16), wo_ref[...])
    y = DEEPNORM_ALPHA * x_ref[...] + mix
    out_ref[...] = _layer_tail(y, p_ref[...], ln_ref, w1_ref, w2_ref, gw_ref, gb_ref, pp_ref)


def _gdn_layer(o, z, x, p, layer, nw, wo, tail_w):
    B, S, D = x.shape
    ple = p.shape[-1]
    tm = ROW_TILE
    row_spec = pl.BlockSpec((None, tm, D), lambda b, i: (b, i, 0))
    in_specs = [
        row_spec, row_spec, row_spec,
        pl.BlockSpec((None, None, tm, ple), lambda b, i: (layer, b, i, 0)),
        _const_spec(nw.shape), _const_spec(wo.shape),
    ] + _tail_specs(D, tail_w[1].shape[1], ple)
    return pl.pallas_call(
        _gdn_layer_kernel,
        grid=(B, S // tm),
        in_specs=in_specs,
        out_specs=row_spec,
        out_shape=jax.ShapeDtypeStruct((B, S, D), F32),
        compiler_params=pltpu.CompilerParams(
            dimension_semantics=("arbitrary", "arbitrary"), vmem_limit_bytes=VMEM_LIMIT),
        name="gdn_layer",
    )(o, z, x, p, nw, wo, *tail_w)


def _tail_weights(i, ln_gain, ln_bias, mlp_w1, mlp_w2, ple_gate_w, ple_gate_b, ple_proj):
    ln = jnp.stack([ln_gain[i, 0], ln_bias[i, 0], ln_gain[i, 1], ln_bias[i, 1]])
    return (ln, mlp_w1[i].astype(BF16), mlp_w2[i].astype(BF16), ple_gate_w[i].astype(BF16),
            ple_gate_b[i][None, :], ple_proj[i].astype(BF16))


def _pad_lanes(vec, offset):
    return jnp.zeros((1, LANES), F32).at[0, offset:offset + vec.shape[0]].set(vec)


def kernel(x, p, ln_gain, ln_bias, pool_w, pool_b, pool_scale, gdn_w_in, gdn_conv, gdn_a_log, gdn_dt_bias,
           gdn_norm_w, gdn_w_out, mlp_w1, mlp_w2, ple_gate_w, ple_gate_b, ple_proj):
    B, S, D = x.shape
    assert S % ROW_TILE == 0 and S % DELTA_TILE == 0 and ROW_TILE % CHUNK == 0 and DELTA_TILE % CHUNK == 0
    assert D == GDN_HEADS * HEAD_DIM and D % len(POOL_WINDOWS) == 0
    tail_args = (ln_gain, ln_bias, mlp_w1, mlp_w2, ple_gate_w, ple_gate_b, ple_proj)

    x = _pool_layer(x, p, 0, pool_w[0].astype(BF16), pool_b[0], pool_scale[0][None, :],
                    _tail_weights(0, *tail_args))

    w_in = gdn_w_in[0]
    conv_dim = 3 * D
    wqkv = w_in[:, :conv_dim].astype(BF16)
    wz = w_in[:, conv_dim:conv_dim + D].astype(BF16)
    wba = jnp.pad(w_in[:, conv_dim + D:], ((0, 0), (0, LANES - 2 * GDN_HEADS))).astype(BF16)
    alog = _pad_lanes(gdn_a_log[0], GDN_HEADS)
    dtb = _pad_lanes(gdn_dt_bias[0], GDN_HEADS)
    q, k, v, z, beta, gc = _gdn_proj(x, wqkv, wz, wba, gdn_conv[0], alog, dtb)
    o = _gdn_delta(q, k, v, beta, gc)
    nw = jnp.tile(gdn_norm_w[0], GDN_HEADS)[None, :]
    return _gdn_layer(o, z, x, p, 1, nw, gdn_w_out[0].astype(BF16), _tail_weights(1, *tail_args))
```

```python
import functools

import jax
import jax.numpy as jnp
from jax import lax
from jax.experimental import pallas as pl
from jax.experimental.pallas import tpu as pltpu

F32 = jnp.float32
BF16 = jnp.bfloat16

DEPTH = 2
POOL_WINDOWS = (2, 4, 8, 16)
POOL_HALO = 16
GDN_HEADS = 8
HEAD_DIM = 128
CONV_WIDTH = 4
CONV_HALO = 8
CHUNK = 64
DEEPNORM_ALPHA = (2.0 * DEPTH) ** 0.25
LN_EPS = 1e-5
RMS_EPS = 1e-6
L2_EPS = 1e-6

LANES = 128
ROW_TILE = 512
SUB_TILE = 256
DELTA_TILE = 256
DELTA_CHUNKS = 2
FF_CHUNK = 512
PROJ_COLS = 256
VMEM_LIMIT = 56 * 1024 * 1024


def _dot(a, b):
    return jnp.dot(a, b, preferred_element_type=F32)


def _dot_nt(a, b):
    return lax.dot_general(a, b, (((1,), (1,)), ((), ())), preferred_element_type=F32)


def _split_bf16(x):
    hi = x.astype(BF16)
    lo = (x - hi.astype(F32)).astype(BF16)
    return hi, lo


def _dot_split(x, w_bf16):
    hi, lo = _split_bf16(x)
    return _dot(hi, w_bf16) + _dot(lo, w_bf16)


def _layer_norm(y, g, b):
    mu = jnp.mean(y, axis=-1, keepdims=True)
    d = y - mu
    var = jnp.mean(d * d, axis=-1, keepdims=True)
    return d * lax.rsqrt(var + LN_EPS) * g + b


class _Staged:
    def __init__(self, stages):
        self._stages = stages
        self._done = False
        self._value = None

    def step(self):
        if not self._done:
            try:
                next(self._stages)
            except StopIteration as stop:
                self._done, self._value = True, stop.value

    def finish(self):
        while not self._done:
            self.step()
        return self._value


def _layer_tail(mixer, p_ref, ln_ref, w1_ref, w2_ref, gw_ref, gb_ref, pp_ref, o_ref):
    n_sub = o_ref.shape[0] // SUB_TILE
    n_chunks = w1_ref.shape[1] // FF_CHUNK

    def first_norm(y):
        x1 = _layer_norm(y, ln_ref[0:1, :], ln_ref[1:2, :])
        return x1, x1.astype(BF16)

    def hidden(x1b, c):
        return _dot(x1b, w1_ref[:, c * FF_CHUNK:(c + 1) * FF_CHUNK])

    sub_rows = [slice(s * SUB_TILE, (s + 1) * SUB_TILE) for s in range(n_sub)]
    x1, x1b = first_norm(_Staged(mixer(sub_rows[0])).finish())
    for s, rows in enumerate(sub_rows):
        upcoming = _Staged(mixer(sub_rows[s + 1])) if s + 1 < n_sub else None
        acc = None
        h_next = hidden(x1b, 0)
        for c in range(n_chunks):
            h = h_next
            if c + 1 < n_chunks:
                h_next = hidden(x1b, c + 1)
            hb = jnp.square(jnp.maximum(h, 0.0)).astype(BF16)
            t = _dot(hb, w2_ref[c * FF_CHUNK:(c + 1) * FF_CHUNK, :])
            acc = t if acc is None else acc + t
            if upcoming is not None:
                if c < n_chunks - 2:
                    upcoming.step()
                elif c == n_chunks - 2:
                    x1_up, x1b_up = first_norm(upcoming.finish())
        gate = jax.nn.sigmoid(_dot(x1b, gw_ref[...]) + gb_ref[...])
        pe = gate * _dot(p_ref[rows, :].astype(BF16), pp_ref[...])
        y2 = DEEPNORM_ALPHA * x1 + acc + pe
        o_ref[rows, :] = _layer_norm(y2, ln_ref[2:3, :], ln_ref[3:4, :])
        if upcoming is not None:
            x1, x1b = x1_up, x1b_up


def _pool_layer_kernel(x_ref, halo_ref, p_ref, pw_ref, pb_ref, ps_ref, ln_ref, w1_ref, w2_ref,
                       gw_ref, gb_ref, pp_ref, o_ref, xe_ref):
    i = pl.program_id(1)
    tm = x_ref.shape[0]
    group = pw_ref.shape[1]
    xe_ref[0:POOL_HALO, :] = jnp.where(i == 0, 0.0, halo_ref[...])
    xe_ref[POOL_HALO:, :] = x_ref[...]

    def mixer(rows):
        n = rows.stop - rows.start
        pos = (lax.broadcasted_iota(jnp.int32, (n, 1), 0) + (i * tm + rows.start + 1)).astype(F32)
        x = x_ref[rows, :]
        mixes = []
        for gi, w in enumerate(POOL_WINDOWS):
            sl = slice(gi * group, (gi + 1) * group)
            s = xe_ref[rows.start:rows.stop + POOL_HALO, sl]
            m = 1
            while m < w:
                s = s + pltpu.roll(s, m, 0)
                m *= 2
            mean = s[POOL_HALO:] * (1.0 / jnp.minimum(pos, float(w)))
            pooled = (mean - x[:, sl]).astype(BF16)
            mixes.append(_dot(pooled, pw_ref[gi]) + pb_ref[gi:gi + 1, :])
            yield
        mix = jnp.concatenate(mixes, axis=-1) * ps_ref[...]
        return DEEPNORM_ALPHA * x + mix

    _layer_tail(mixer, p_ref, ln_ref, w1_ref, w2_ref, gw_ref, gb_ref, pp_ref, o_ref)


def _const_spec(shape):
    nd = len(shape)
    return pl.BlockSpec(shape, lambda b, i: (0,) * nd, pipeline_mode=pl.Buffered(1))


def _tail_specs(d, d_ff, ple):
    return [_const_spec((4, d)), _const_spec((d, d_ff)), _const_spec((d_ff, d)),
            _const_spec((d, d)), _const_spec((1, d)), _const_spec((ple, d))]


def _pool_layer(x, p, layer, pw, pb, ps, tail_w):
    B, S, D = x.shape
    ple = p.shape[-1]
    tm = ROW_TILE
    halo_blocks = tm // POOL_HALO
    row_spec = pl.BlockSpec((None, tm, D), lambda b, i: (b, i, 0))
    in_specs = [
        row_spec,
        pl.BlockSpec((None, POOL_HALO, D), lambda b, i: (b, jnp.maximum(i * halo_blocks - 1, 0), 0)),
        pl.BlockSpec((None, None, tm, ple), lambda b, i: (layer, b, i, 0)),
        _const_spec(pw.shape), _const_spec(pb.shape), _const_spec(ps.shape),
    ] + _tail_specs(D, tail_w[1].shape[1], ple)
    return pl.pallas_call(
        _pool_layer_kernel,
        grid=(B, S // tm),
        in_specs=in_specs,
        out_specs=row_spec,
        out_shape=jax.ShapeDtypeStruct((B, S, D), F32),
        scratch_shapes=[pltpu.VMEM((tm + POOL_HALO, D), F32)],
        compiler_params=pltpu.CompilerParams(
            dimension_semantics=("arbitrary", "arbitrary"), vmem_limit_bytes=VMEM_LIMIT),
        name="pool_layer",
    )(x, x, p, pw, pb, ps, *tail_w)


def _softplus(x):
    return jnp.maximum(x, 0.0) + jnp.log1p(jnp.exp(-jnp.abs(x)))


def _gdn_proj_kernel(x_ref, wqkv_ref, wz_ref, wba_ref, cw_ref, alog_ref, dtb_ref, tri_ref, eb_ref, ea_ref,
                     q_ref, k_ref, v_ref, z_ref, beta_ref, gc_ref, carry_ref):
    i = pl.program_id(1)
    tm, d = x_ref.shape

    @pl.when(i == 0)
    def _():
        carry_ref[...] = jnp.zeros_like(carry_ref)

    xb = x_ref[...].astype(BF16)
    n_col = d // PROJ_COLS
    col_chunks = [slice(c * PROJ_COLS, (c + 1) * PROJ_COLS) for c in range(n_col)]

    def project(part, cc):
        return _dot(xb, wqkv_ref[:, part * d + cc * PROJ_COLS:part * d + (cc + 1) * PROJ_COLS])

    def conv_act(part, cc, pre):
        lc = col_chunks[cc]
        cs = slice(part * d + lc.start, part * d + lc.stop)
        ext = jnp.concatenate([carry_ref[part, :, lc], pre], axis=0)
        carry_ref[part, :, lc] = pre[tm - CONV_HALO:, :]
        conv = pre * cw_ref[CONV_WIDTH - 1:CONV_WIDTH, cs]
        for j in range(CONV_WIDTH - 1):
            back = CONV_WIDTH - 1 - j
            conv = conv + pltpu.roll(ext, back, 0)[CONV_HALO:] * cw_ref[j:j + 1, cs]
        return conv * jax.nn.sigmoid(conv)

    def store_normalized(out_ref, cc, act, scale):
        for h in range(PROJ_COLS // HEAD_DIM):
            a = act[:, h * HEAD_DIM:(h + 1) * HEAD_DIM]
            r = lax.rsqrt(jnp.sum(a * a, axis=-1, keepdims=True) + L2_EPS)
            col = col_chunks[cc].start + h * HEAD_DIM
            out_ref[:, col:col + HEAD_DIM] = a * (r * scale)

    def gates():
        ba = _dot(xb, wba_ref[...])
        beta_hi, beta_lo = _split_bf16(jax.nn.sigmoid(ba))
        g_hi, g_lo = _split_bf16(-jnp.exp(alog_ref[...]) * _softplus(ba + dtb_ref[...]))
        yield
        gc = _dot(tri_ref[...], g_hi) + _dot(tri_ref[...], g_lo)
        gc_hi, gc_lo = _split_bf16(gc)
        yield
        for lc in col_chunks:
            z_ref[:, lc] = _dot(xb, wz_ref[:, lc])
            yield
            beta_ref[:, lc] = _dot(beta_hi, eb_ref[:, lc]) + _dot(beta_lo, eb_ref[:, lc])
            gc_ref[:, lc] = _dot(gc_hi, ea_ref[:, lc]) + _dot(gc_lo, ea_ref[:, lc])
            yield

    gate_stages = _Staged(gates())
    jobs = [(part, cc) for part in range(3) for cc in range(n_col)]
    pre_next = project(*jobs[0])
    for idx, (part, cc) in enumerate(jobs):
        pre = pre_next
        if idx + 1 < len(jobs):
            pre_next = project(*jobs[idx + 1])
        gate_stages.step()
        act = conv_act(part, cc, pre)
        if part == 0:
            store_normalized(q_ref, cc, act, HEAD_DIM ** -0.5)
        elif part == 1:
            store_normalized(k_ref, cc, act, 1.0)
        else:
            v_ref[:, col_chunks[cc]] = act
    gate_stages.finish()


def _gdn_proj(x, wqkv, wz, wba, cw, alog, dtb):
    B, S, D = x.shape
    tm = ROW_TILE
    r = jnp.arange(tm)
    tri = ((r[:, None] >= r[None, :]) & (r[:, None] // CHUNK == r[None, :] // CHUNK)).astype(BF16)
    lane_head = jnp.arange(GDN_HEADS * HEAD_DIM) // HEAD_DIM
    src = jnp.arange(LANES)
    eb = (src[:, None] == lane_head[None, :]).astype(BF16)
    ea = (src[:, None] == lane_head[None, :] + GDN_HEADS).astype(BF16)
    row_spec = pl.BlockSpec((None, tm, D), lambda b, i: (b, i, 0))
    consts = (wqkv, wz, wba, cw, alog, dtb, tri, eb, ea)
    return pl.pallas_call(
        _gdn_proj_kernel,
        grid=(B, S // tm),
        in_specs=[row_spec] + [_const_spec(c.shape) for c in consts],
        out_specs=[row_spec] * 6,
        out_shape=[jax.ShapeDtypeStruct((B, S, D), F32)] * 6,
        scratch_shapes=[pltpu.VMEM((3, CONV_HALO, D), F32)],
        compiler_params=pltpu.CompilerParams(
            dimension_semantics=("arbitrary", "arbitrary"), vmem_limit_bytes=VMEM_LIMIT),
        name="gdn_proj",
    )(x, *consts)


def _gdn_delta_kernel(q_ref, k_ref, v_ref, beta_ref, gc_ref, o_ref, state_ref):
    i = pl.program_id(0)
    nb, ts, _ = q_ref.shape
    c = CHUNK
    dh = HEAD_DIM
    assert 2 * c == dh
    half = GDN_HEADS // 2
    n_pairs = nb * half
    pairs = range(DELTA_CHUNKS * n_pairs)

    @pl.when(i == 0)
    def _():
        state_ref[...] = jnp.zeros_like(state_ref)

    ii = lax.broadcasted_iota(jnp.int32, (c, dh), 0)
    lane = lax.broadcasted_iota(jnp.int32, (c, dh), 1)
    left = lane < c
    jj = jnp.where(left, lane, lane - c)
    causal = ii >= jj
    strict = ii > jj
    eye = jnp.where(ii == jj, 1.0, 0.0)
    left_sq = lax.broadcasted_iota(jnp.int32, (dh, dh), 1) < c
    head_a = lax.broadcasted_iota(jnp.int32, (c, 2 * dh), 1) < dh

    def block_diag(x):
        return jnp.concatenate([jnp.where(left, x, 0.0), jnp.where(left, 0.0, x)], axis=0)

    def stack_heads(x):
        return jnp.concatenate([x[:, :dh], x[:, dh:]], axis=0)

    def chunk_body(ci, carry):
        rows = [pl.ds(pl.multiple_of((ci * DELTA_CHUNKS + j) * c, c), c) for j in range(DELTA_CHUNKS)]
        at = [(p % n_pairs // half, rows[p // n_pairs], slice(p % half * 2 * dh, (p % half + 1) * 2 * dh))
              for p in pairs]
        gc = [gc_ref[at[p]] for p in pairs]
        k = [k_ref[at[p]] for p in pairs]
        kb = [k[p] * beta_ref[at[p]] for p in pairs]
        decay = []
        for p in pairs:
            gi = jnp.where(left, gc[p][:, :dh], gc[p][:, dh:])
            gj = stack_heads(gc[p]).T[:c]
            decay.append(jnp.exp(jnp.where(causal, gi - gj, -jnp.inf)))
        akq = []
        for p in pairs:
            k_diag = jnp.concatenate([jnp.where(head_a, k[p], 0.0), jnp.where(head_a, 0.0, k[p])], axis=0)
            lhs = jnp.concatenate([kb[p], q_ref[at[p]]], axis=0)
            akq.append(_dot_nt(lhs.astype(BF16), k_diag.astype(BF16)))
        n = [jnp.where(strict, akq[p][:c] * decay[p], 0.0) for p in pairs]
        qk = [akq[p][c:] * decay[p] for p in pairs]

        t = [eye - n[p] for p in pairs]
        pw = [_dot(n[p].astype(BF16), block_diag(n[p]).astype(BF16)) for p in pairs]
        terms = 2
        while terms < c:
            pw_diag = [block_diag(pw[p]).astype(BF16) for p in pairs]
            if 2 * terms < c:
                both = [_dot(jnp.concatenate([pw[p], t[p]], axis=0).astype(BF16), pw_diag[p]) for p in pairs]
                pw = [both[p][:c] for p in pairs]
                t = [t[p] + both[p][c:] for p in pairs]
            else:
                t = [t[p] + _dot(t[p].astype(BF16), pw_diag[p]) for p in pairs]
            terms *= 2

        eg = [jnp.exp(gc[p]) for p in pairs]
        uw = []
        for p in pairs:
            vb = v_ref[at[p]] * beta_ref[at[p]]
            kbg = kb[p] * eg[p]
            rhs = jnp.concatenate([jnp.concatenate([vb[:, :dh], kbg[:, :dh]], axis=1),
                                   jnp.concatenate([vb[:, dh:], kbg[:, dh:]], axis=1)], axis=0)
            uw.append(_dot(block_diag(t[p]).astype(BF16), rhs.astype(BF16)).astype(BF16))
        f = {}

        def f_stage(j):
            for p in range(j * n_pairs, (j + 1) * n_pairs):
                kd = k[p] * jnp.exp(gc[p][c - 1:c, :] - gc[p])
                kd_t = stack_heads(kd).T
                lhs = jnp.concatenate(
                    [block_diag(qk[p]), jnp.where(left_sq, kd_t, 0.0), jnp.where(left_sq, 0.0, kd_t)], axis=0)
                f[p] = _dot(lhs.astype(BF16), uw[p])

        def f_parts(p, s):
            return f[p][s * c:(s + 1) * c], f[p][2 * c + s * dh:2 * c + (s + 1) * dh]

        f_stage(0)
        for j in range(DELTA_CHUNKS):
            r = {}
            for p in range(j * n_pairs, (j + 1) * n_pairs):
                qe = q_ref[at[p]] * eg[p]
                for s in range(2):
                    h = 2 * (p % n_pairs) + s
                    f_qk, f_kd = f_parts(p, s)
                    m = jnp.concatenate([-f_kd[:, dh:], qe[:, s * dh:(s + 1) * dh] - f_qk[:, dh:]], axis=0)
                    r[h] = _dot(m.astype(BF16), state_ref[h].astype(BF16))
            if j + 1 < DELTA_CHUNKS:
                f_stage(j + 1)
            for p in range(j * n_pairs, (j + 1) * n_pairs):
                b, chunk_rows, cols = at[p]
                for s in range(2):
                    h = 2 * (p % n_pairs) + s
                    f_qk, f_kd = f_parts(p, s)
                    decay_last = eg[p][c - 1:c, s * dh:(s + 1) * dh]
                    state_ref[h] = decay_last * state_ref[h] + r[h][:dh] + f_kd[:, :dh]
                    o_ref[b, chunk_rows, cols.start + s * dh:cols.start + (s + 1) * dh] = r[h][dh:] + f_qk[:, :dh]
        return carry

    lax.fori_loop(0, ts // (c * DELTA_CHUNKS), chunk_body, 0)


def _gdn_delta(q, k, v, beta, gc):
    B, S, D = q.shape
    ts = DELTA_TILE
    row_spec = pl.BlockSpec((B, ts, D), lambda i: (0, i, 0))
    return pl.pallas_call(
        _gdn_delta_kernel,
        grid=(S // ts,),
        in_specs=[row_spec] * 5,
        out_specs=row_spec,
        out_shape=jax.ShapeDtypeStruct((B, S, D), F32),
        scratch_shapes=[pltpu.VMEM((B * GDN_HEADS, HEAD_DIM, HEAD_DIM), F32)],
        compiler_params=pltpu.CompilerParams(dimension_semantics=("arbitrary",), vmem_limit_bytes=VMEM_LIMIT),
        name="gdn_delta",
    )(q, k, v, beta, gc)


def _gdn_layer_kernel(o_ref, z_ref, x_ref, p_ref, nw_ref, wo_ref, ln_ref, w1_ref, w2_ref,
                      gw_ref, gb_ref, pp_ref, out_ref):
    def mixer(rows):
        z = z_ref[rows, :]
        normed = []
        for h in range(GDN_HEADS):
            oh = o_ref[rows, h * HEAD_DIM:(h + 1) * HEAD_DIM]
            normed.append(oh * lax.rsqrt(jnp.mean(oh * oh, axis=-1, keepdims=True) + RMS_EPS))
            if h % 2 == 1:
                yield
        gated = jnp.concatenate(normed, axis=-1) * nw_ref[...] * (z * jax.nn.sigmoid(z))
        yield
        mix = _dot(gated.astype(BF16), wo_ref[...])
        return DEEPNORM_ALPHA * x_ref[rows, :] + mix

    _layer_tail(mixer, p_ref, ln_ref, w1_ref, w2_ref, gw_ref, gb_ref, pp_ref, out_ref)


def _gdn_layer(o, z, x, p, layer, nw, wo, tail_w):
    B, S, D = x.shape
    ple = p.shape[-1]
    tm = ROW_TILE
    row_spec = pl.BlockSpec((None, tm, D), lambda b, i: (b, i, 0))
    in_specs = [
        row_spec, row_spec, row_spec,
        pl.BlockSpec((None, None, tm, ple), lambda b, i: (layer, b, i, 0)),
        _const_spec(nw.shape), _const_spec(wo.shape),
    ] + _tail_specs(D, tail_w[1].shape[1], ple)
    return pl.pallas_call(
        _gdn_layer_kernel,
        grid=(B, S // tm),
        in_specs=in_specs,
        out_specs=row_spec,
        out_shape=jax.ShapeDtypeStruct((B, S, D), F32),
        compiler_params=pltpu.CompilerParams(
            dimension_semantics=("arbitrary", "arbitrary"), vmem_limit_bytes=VMEM_LIMIT),
        name="gdn_layer",
    )(o, z, x, p, nw, wo, *tail_w)


def _tail_weights(i, ln_gain, ln_bias, mlp_w1, mlp_w2, ple_gate_w, ple_gate_b, ple_proj):
    ln = jnp.stack([ln_gain[i, 0], ln_bias[i, 0], ln_gain[i, 1], ln_bias[i, 1]])
    return (ln, mlp_w1[i].astype(BF16), mlp_w2[i].astype(BF16), ple_gate_w[i].astype(BF16),
            ple_gate_b[i][None, :], ple_proj[i].astype(BF16))


def _pad_lanes(vec, offset):
    return jnp.zeros((1, LANES), F32).at[0, offset:offset + vec.shape[0]].set(vec)


def kernel(x, p, ln_gain, ln_bias, pool_w, pool_b, pool_scale, gdn_w_in, gdn_conv, gdn_a_log, gdn_dt_bias,
           gdn_norm_w, gdn_w_out, mlp_w1, mlp_w2, ple_gate_w, ple_gate_b, ple_proj):
    B, S, D = x.shape
    assert S % ROW_TILE == 0 and S % DELTA_TILE == 0 and ROW_TILE % CHUNK == 0
    assert DELTA_TILE % (CHUNK * DELTA_CHUNKS) == 0 and ROW_TILE % SUB_TILE == 0
    assert D == GDN_HEADS * HEAD_DIM and D % len(POOL_WINDOWS) == 0
    tail_args = (ln_gain, ln_bias, mlp_w1, mlp_w2, ple_gate_w, ple_gate_b, ple_proj)

    x = _pool_layer(x, p, 0, pool_w[0].astype(BF16), pool_b[0], pool_scale[0][None, :],
                    _tail_weights(0, *tail_args))

    w_in = gdn_w_in[0]
    conv_dim = 3 * D
    wqkv = w_in[:, :conv_dim].astype(BF16)
    wz = w_in[:, conv_dim:conv_dim + D].astype(BF16)
    wba = jnp.pad(w_in[:, conv_dim + D:], ((0, 0), (0, LANES - 2 * GDN_HEADS))).astype(BF16)
    alog = _pad_lanes(gdn_a_log[0], GDN_HEADS)
    dtb = _pad_lanes(gdn_dt_bias[0], GDN_HEADS)
    q, k, v, z, beta, gc = _gdn_proj(x, wqkv, wz, wba, gdn_conv[0], alog, dtb)
    o = _gdn_delta(q, k, v, beta, gc)
    nw = jnp.tile(gdn_norm_w[0], GDN_HEADS)[None, :]
    return _gdn_layer(o, z, x, p, 1, nw, gdn_w_out[0].astype(BF16), _tail_weights(1, *tail_args))
```

```python
import functools

import jax
import jax.numpy as jnp
from jax import lax
from jax.experimental import pallas as pl
from jax.experimental.pallas import tpu as pltpu

F32 = jnp.float32
BF16 = jnp.bfloat16

DEPTH = 2
POOL_WINDOWS = (2, 4, 8, 16)
POOL_HALO = 16
GDN_HEADS = 8
HEAD_DIM = 128
CONV_WIDTH = 4
CONV_HALO = 8
CHUNK = 64
DEEPNORM_ALPHA = (2.0 * DEPTH) ** 0.25
LN_EPS = 1e-5
RMS_EPS = 1e-6
L2_EPS = 1e-6

LANES = 128
ROW_TILE = 512
SUB_TILE = 256
DELTA_TILE = 256
DELTA_CHUNKS = 2
FF_CHUNK = 512
PROJ_COLS = 256
VMEM_LIMIT = 56 * 1024 * 1024


def _dot(a, b):
    return jnp.dot(a, b, preferred_element_type=F32)


def _dot_nt(a, b):
    return lax.dot_general(a, b, (((1,), (1,)), ((), ())), preferred_element_type=F32)


def _split_bf16(x):
    hi = x.astype(BF16)
    lo = (x - hi.astype(F32)).astype(BF16)
    return hi, lo


def _dot_split(x, w_bf16):
    hi, lo = _split_bf16(x)
    return _dot(hi, w_bf16) + _dot(lo, w_bf16)


def _layer_norm(y, g, b):
    mu = jnp.mean(y, axis=-1, keepdims=True)
    d = y - mu
    var = jnp.mean(d * d, axis=-1, keepdims=True)
    return d * lax.rsqrt(var + LN_EPS) * g + b


class _Staged:
    def __init__(self, stages):
        self._stages = stages
        self._done = False
        self._value = None

    def step(self):
        if not self._done:
            try:
                next(self._stages)
            except StopIteration as stop:
                self._done, self._value = True, stop.value

    def finish(self):
        while not self._done:
            self.step()
        return self._value


def _layer_tail(mixer, p_ref, ln_ref, w1_ref, w2_ref, gw_ref, gb_ref, pp_ref, o_ref):
    n_sub = o_ref.shape[0] // SUB_TILE
    n_chunks = w1_ref.shape[1] // FF_CHUNK

    def first_norm(y):
        x1 = _layer_norm(y, ln_ref[0:1, :], ln_ref[1:2, :])
        return x1, x1.astype(BF16)

    def hidden(x1b, c):
        return _dot(x1b, w1_ref[:, c * FF_CHUNK:(c + 1) * FF_CHUNK])

    sub_rows = [slice(s * SUB_TILE, (s + 1) * SUB_TILE) for s in range(n_sub)]
    x1, x1b = first_norm(_Staged(mixer(sub_rows[0])).finish())
    for s, rows in enumerate(sub_rows):
        upcoming = _Staged(mixer(sub_rows[s + 1])) if s + 1 < n_sub else None
        acc = None
        h_next = hidden(x1b, 0)
        for c in range(n_chunks):
            h = h_next
            if c + 1 < n_chunks:
                h_next = hidden(x1b, c + 1)
            hb = jnp.square(jnp.maximum(h, 0.0)).astype(BF16)
            t = _dot(hb, w2_ref[c * FF_CHUNK:(c + 1) * FF_CHUNK, :])
            acc = t if acc is None else acc + t
            if upcoming is not None:
                if c < n_chunks - 2:
                    upcoming.step()
                elif c == n_chunks - 2:
                    x1_up, x1b_up = first_norm(upcoming.finish())
        gate = jax.nn.sigmoid(_dot(x1b, gw_ref[...]) + gb_ref[...])
        pe = gate * _dot(p_ref[rows, :].astype(BF16), pp_ref[...])
        y2 = DEEPNORM_ALPHA * x1 + acc + pe
        o_ref[rows, :] = _layer_norm(y2, ln_ref[2:3, :], ln_ref[3:4, :])
        if upcoming is not None:
            x1, x1b = x1_up, x1b_up


def _pool_layer_kernel(x_ref, halo_ref, p_ref, pw_ref, pb_ref, ps_ref, ln_ref, w1_ref, w2_ref,
                       gw_ref, gb_ref, pp_ref, o_ref, xe_ref):
    i = pl.program_id(1)
    tm = x_ref.shape[0]
    group = pw_ref.shape[1]
    xe_ref[0:POOL_HALO, :] = jnp.where(i == 0, 0.0, halo_ref[...])
    xe_ref[POOL_HALO:, :] = x_ref[...]

    def mixer(rows):
        n = rows.stop - rows.start
        pos = (lax.broadcasted_iota(jnp.int32, (n, 1), 0) + (i * tm + rows.start + 1)).astype(F32)
        x = x_ref[rows, :]
        mixes = []
        for gi, w in enumerate(POOL_WINDOWS):
            sl = slice(gi * group, (gi + 1) * group)
            s = xe_ref[rows.start:rows.stop + POOL_HALO, sl]
            m = 1
            while m < w:
                s = s + pltpu.roll(s, m, 0)
                m *= 2
            mean = s[POOL_HALO:] * (1.0 / jnp.minimum(pos, float(w)))
            pooled = (mean - x[:, sl]).astype(BF16)
            mixes.append(_dot(pooled, pw_ref[gi]) + pb_ref[gi:gi + 1, :])
            yield
        mix = jnp.concatenate(mixes, axis=-1) * ps_ref[...]
        return DEEPNORM_ALPHA * x + mix

    _layer_tail(mixer, p_ref, ln_ref, w1_ref, w2_ref, gw_ref, gb_ref, pp_ref, o_ref)


def _fixed_spec(block_shape, index):
    return pl.BlockSpec(block_shape, lambda b, i: index, pipeline_mode=pl.Buffered(1))


def _const_spec(shape):
    return _fixed_spec(shape, (0,) * len(shape))


def _layer_spec(shape, layer):
    return _fixed_spec((None,) + tuple(shape), (layer,) + (0,) * len(shape))


def _tail_specs(layer, d, d_ff, ple):
    shapes = [(4, d), (d, d_ff), (d_ff, d), (d, d), (1, d), (ple, d)]
    return [_layer_spec(s, layer) for s in shapes]


def _pool_layer(x, p, layer, pw, pb, ps, tail_w):
    B, S, D = x.shape
    ple = p.shape[-1]
    tm = ROW_TILE
    halo_blocks = tm // POOL_HALO
    row_spec = pl.BlockSpec((None, tm, D), lambda b, i: (b, i, 0))
    in_specs = [
        row_spec,
        pl.BlockSpec((None, POOL_HALO, D), lambda b, i: (b, jnp.maximum(i * halo_blocks - 1, 0), 0)),
        pl.BlockSpec((None, None, tm, ple), lambda b, i: (layer, b, i, 0)),
        _layer_spec(pw.shape[1:], 0), _layer_spec(pb.shape[1:], 0), _const_spec(ps.shape),
    ] + _tail_specs(layer, D, tail_w[1].shape[-1], ple)
    return pl.pallas_call(
        _pool_layer_kernel,
        grid=(B, S // tm),
        in_specs=in_specs,
        out_specs=row_spec,
        out_shape=jax.ShapeDtypeStruct((B, S, D), F32),
        scratch_shapes=[pltpu.VMEM((tm + POOL_HALO, D), F32)],
        compiler_params=pltpu.CompilerParams(
            dimension_semantics=("arbitrary", "arbitrary"), vmem_limit_bytes=VMEM_LIMIT),
        name="pool_layer",
    )(x, x, p, pw, pb, ps, *tail_w)


def _softplus(x):
    return jnp.maximum(x, 0.0) + jnp.log1p(jnp.exp(-jnp.abs(x)))


def _gdn_proj_kernel(x_ref, wqkv_ref, wz_ref, cw_ref, wba_ref, alog_ref, dtb_ref, tri_ref, eb_ref, ea_ref,
                     q_ref, k_ref, v_ref, z_ref, beta_ref, gc_ref, carry_ref):
    i = pl.program_id(1)
    tm, d = x_ref.shape

    @pl.when(i == 0)
    def _():
        carry_ref[...] = jnp.zeros_like(carry_ref)

    xb = x_ref[...].astype(BF16)
    n_col = d // PROJ_COLS
    col_chunks = [slice(c * PROJ_COLS, (c + 1) * PROJ_COLS) for c in range(n_col)]

    def project(part, cc):
        return _dot(xb, wqkv_ref[:, part * d + cc * PROJ_COLS:part * d + (cc + 1) * PROJ_COLS])

    def conv_act(part, cc, pre):
        lc = col_chunks[cc]
        cs = slice(part * d + lc.start, part * d + lc.stop)
        ext = jnp.concatenate([carry_ref[part, :, lc], pre], axis=0)
        carry_ref[part, :, lc] = pre[tm - CONV_HALO:, :]
        conv = pre * cw_ref[CONV_WIDTH - 1:CONV_WIDTH, cs]
        for j in range(CONV_WIDTH - 1):
            back = CONV_WIDTH - 1 - j
            conv = conv + pltpu.roll(ext, back, 0)[CONV_HALO:] * cw_ref[j:j + 1, cs]
        return conv * jax.nn.sigmoid(conv)

    def store_normalized(out_ref, cc, act, scale):
        for h in range(PROJ_COLS // HEAD_DIM):
            a = act[:, h * HEAD_DIM:(h + 1) * HEAD_DIM]
            r = lax.rsqrt(jnp.sum(a * a, axis=-1, keepdims=True) + L2_EPS)
            col = col_chunks[cc].start + h * HEAD_DIM
            out_ref[:, col:col + HEAD_DIM] = (a * (r * scale)).astype(out_ref.dtype)

    def gates():
        ba = _dot(xb, wba_ref[...])
        beta = jax.nn.sigmoid(ba).astype(beta_ref.dtype)
        g_hi, g_lo = _split_bf16(-jnp.exp(alog_ref[...]) * _softplus(ba + dtb_ref[...]))
        yield
        gc = _dot(tri_ref[...], g_hi) + _dot(tri_ref[...], g_lo)
        gc_hi, gc_lo = _split_bf16(gc)
        yield
        for lc in col_chunks:
            z_ref[:, lc] = _dot(xb, wz_ref[:, lc]).astype(z_ref.dtype)
            yield
            beta_ref[:, lc] = _dot(beta, eb_ref[:, lc]).astype(beta_ref.dtype)
            gc_ref[:, lc] = _dot(gc_hi, ea_ref[:, lc]) + _dot(gc_lo, ea_ref[:, lc])
            yield

    gate_stages = _Staged(gates())
    jobs = [(part, cc) for part in range(3) for cc in range(n_col)]
    pre_next = project(*jobs[0])
    for idx, (part, cc) in enumerate(jobs):
        pre = pre_next
        if idx + 1 < len(jobs):
            pre_next = project(*jobs[idx + 1])
        gate_stages.step()
        act = conv_act(part, cc, pre)
        if part == 0:
            store_normalized(q_ref, cc, act, HEAD_DIM ** -0.5)
        elif part == 1:
            store_normalized(k_ref, cc, act, 1.0)
        else:
            v_ref[:, col_chunks[cc]] = act.astype(v_ref.dtype)
    gate_stages.finish()


def _gdn_proj(x, w_in, wba, cw, alog, dtb):
    B, S, D = x.shape
    tm = ROW_TILE
    r = jnp.arange(tm)
    tri = ((r[:, None] >= r[None, :]) & (r[:, None] // CHUNK == r[None, :] // CHUNK)).astype(BF16)
    lane_head = jnp.arange(GDN_HEADS * HEAD_DIM) // HEAD_DIM
    src = jnp.arange(LANES)
    eb = (src[:, None] == lane_head[None, :]).astype(BF16)
    ea = (src[:, None] == lane_head[None, :] + GDN_HEADS).astype(BF16)
    row_spec = pl.BlockSpec((None, tm, D), lambda b, i: (b, i, 0))
    consts = (wba, alog, dtb, tri, eb, ea)
    in_specs = [row_spec,
                _fixed_spec((None, D, 3 * D), (0, 0, 0)),
                _fixed_spec((None, D, D), (0, 0, 3)),
                _layer_spec(cw.shape[1:], 0)] + [_const_spec(c.shape) for c in consts]
    out_dtypes = [BF16, BF16, BF16, BF16, BF16, F32]
    return pl.pallas_call(
        _gdn_proj_kernel,
        grid=(B, S // tm),
        in_specs=in_specs,
        out_specs=[row_spec] * 6,
        out_shape=[jax.ShapeDtypeStruct((B, S, D), dt) for dt in out_dtypes],
        scratch_shapes=[pltpu.VMEM((3, CONV_HALO, D), F32)],
        compiler_params=pltpu.CompilerParams(
            dimension_semantics=("arbitrary", "arbitrary"), vmem_limit_bytes=VMEM_LIMIT),
        name="gdn_proj",
    )(x, w_in, w_in, cw, *consts)


def _gdn_delta_kernel(q_ref, k_ref, v_ref, beta_ref, gc_ref, o_ref, state_ref):
    i = pl.program_id(0)
    nb, ts, _ = q_ref.shape
    c = CHUNK
    dh = HEAD_DIM
    assert 2 * c == dh
    half = GDN_HEADS // 2
    n_pairs = nb * half
    pairs = range(DELTA_CHUNKS * n_pairs)

    @pl.when(i == 0)
    def _():
        state_ref[...] = jnp.zeros_like(state_ref)

    ii = lax.broadcasted_iota(jnp.int32, (c, dh), 0)
    lane = lax.broadcasted_iota(jnp.int32, (c, dh), 1)
    left = lane < c
    jj = jnp.where(left, lane, lane - c)
    causal = ii >= jj
    strict = ii > jj
    eye = jnp.where(ii == jj, 1.0, 0.0)
    left_sq = lax.broadcasted_iota(jnp.int32, (dh, dh), 1) < c
    head_a = lax.broadcasted_iota(jnp.int32, (c, 2 * dh), 1) < dh

    def block_diag(x):
        return jnp.concatenate([jnp.where(left, x, 0.0), jnp.where(left, 0.0, x)], axis=0)

    def stack_heads(x):
        return jnp.concatenate([x[:, :dh], x[:, dh:]], axis=0)

    def chunk_body(ci, carry):
        rows = [pl.ds(pl.multiple_of((ci * DELTA_CHUNKS + j) * c, c), c) for j in range(DELTA_CHUNKS)]
        at = [(p % n_pairs // half, rows[p // n_pairs], slice(p % half * 2 * dh, (p % half + 1) * 2 * dh))
              for p in pairs]
        gc = [gc_ref[at[p]] for p in pairs]
        beta = [beta_ref[at[p]].astype(F32) for p in pairs]
        k = [k_ref[at[p]].astype(F32) for p in pairs]
        kb = [k[p] * beta[p] for p in pairs]
        decay = []
        for p in pairs:
            gi = jnp.where(left, gc[p][:, :dh], gc[p][:, dh:])
            gj = stack_heads(gc[p]).T[:c]
            decay.append(jnp.exp(jnp.where(causal, gi - gj, -jnp.inf)))
        akq = []
        for p in pairs:
            kp = k_ref[at[p]]
            k_diag = jnp.concatenate([jnp.where(head_a, kp, 0), jnp.where(head_a, 0, kp)], axis=0)
            lhs = jnp.concatenate([kb[p].astype(BF16), q_ref[at[p]]], axis=0)
            akq.append(_dot_nt(lhs, k_diag))
        n = [jnp.where(strict, akq[p][:c] * decay[p], 0.0) for p in pairs]
        qk = [akq[p][c:] * decay[p] for p in pairs]

        t = [eye - n[p] for p in pairs]
        pw = [_dot(n[p].astype(BF16), block_diag(n[p]).astype(BF16)) for p in pairs]
        terms = 2
        while terms < c:
            pw_diag = [block_diag(pw[p]).astype(BF16) for p in pairs]
            if 2 * terms < c:
                both = [_dot(jnp.concatenate([pw[p], t[p]], axis=0).astype(BF16), pw_diag[p]) for p in pairs]
                pw = [both[p][:c] for p in pairs]
                t = [t[p] + both[p][c:] for p in pairs]
            else:
                t = [t[p] + _dot(t[p].astype(BF16), pw_diag[p]) for p in pairs]
            terms *= 2

        eg = [jnp.exp(gc[p]) for p in pairs]
        uw = []
        for p in pairs:
            vb = v_ref[at[p]].astype(F32) * beta[p]
            kbg = kb[p] * eg[p]
            rhs = jnp.concatenate([jnp.concatenate([vb[:, :dh], kbg[:, :dh]], axis=1),
                                   jnp.concatenate([vb[:, dh:], kbg[:, dh:]], axis=1)], axis=0)
            uw.append(_dot(block_diag(t[p]).astype(BF16), rhs.astype(BF16)).astype(BF16))
        f = {}

        def f_stage(j):
            for p in range(j * n_pairs, (j + 1) * n_pairs):
                kd = k[p] * jnp.exp(gc[p][c - 1:c, :] - gc[p])
                kd_t = stack_heads(kd).T
                lhs = jnp.concatenate(
                    [block_diag(qk[p]), jnp.where(left_sq, kd_t, 0.0), jnp.where(left_sq, 0.0, kd_t)], axis=0)
                f[p] = _dot(lhs.astype(BF16), uw[p])

        def f_parts(p, s):
            return f[p][s * c:(s + 1) * c], f[p][2 * c + s * dh:2 * c + (s + 1) * dh]

        f_stage(0)
        for j in range(DELTA_CHUNKS):
            r = {}
            for p in range(j * n_pairs, (j + 1) * n_pairs):
                qe = q_ref[at[p]].astype(F32) * eg[p]
                for s in range(2):
                    h = 2 * (p % n_pairs) + s
                    f_qk, f_kd = f_parts(p, s)
                    m = jnp.concatenate([-f_kd[:, dh:], qe[:, s * dh:(s + 1) * dh] - f_qk[:, dh:]], axis=0)
                    r[h] = _dot(m.astype(BF16), state_ref[h].astype(BF16))
            if j + 1 < DELTA_CHUNKS:
                f_stage(j + 1)
            for p in range(j * n_pairs, (j + 1) * n_pairs):
                b, chunk_rows, cols = at[p]
                for s in range(2):
                    h = 2 * (p % n_pairs) + s
                    f_qk, f_kd = f_parts(p, s)
                    decay_last = eg[p][c - 1:c, s * dh:(s + 1) * dh]
                    state_ref[h] = decay_last * state_ref[h] + r[h][:dh] + f_kd[:, :dh]
                    o_ref[b, chunk_rows, cols.start + s * dh:cols.start + (s + 1) * dh] = (
                        r[h][dh:] + f_qk[:, :dh]).astype(o_ref.dtype)
        return carry

    lax.fori_loop(0, ts // (c * DELTA_CHUNKS), chunk_body, 0)


def _gdn_delta(q, k, v, beta, gc):
    B, S, D = q.shape
    ts = DELTA_TILE
    row_spec = pl.BlockSpec((B, ts, D), lambda i: (0, i, 0))
    return pl.pallas_call(
        _gdn_delta_kernel,
        grid=(S // ts,),
        in_specs=[row_spec] * 5,
        out_specs=row_spec,
        out_shape=jax.ShapeDtypeStruct((B, S, D), BF16),
        scratch_shapes=[pltpu.VMEM((B * GDN_HEADS, HEAD_DIM, HEAD_DIM), F32)],
        compiler_params=pltpu.CompilerParams(dimension_semantics=("arbitrary",), vmem_limit_bytes=VMEM_LIMIT),
        name="gdn_delta",
    )(q, k, v, beta, gc)


def _gdn_layer_kernel(o_ref, z_ref, x_ref, p_ref, nw_ref, wo_ref, ln_ref, w1_ref, w2_ref,
                      gw_ref, gb_ref, pp_ref, out_ref):
    def mixer(rows):
        z = z_ref[rows, :].astype(F32)
        normed = []
        for h in range(GDN_HEADS):
            oh = o_ref[rows, h * HEAD_DIM:(h + 1) * HEAD_DIM].astype(F32)
            normed.append(oh * lax.rsqrt(jnp.mean(oh * oh, axis=-1, keepdims=True) + RMS_EPS))
            if h % 2 == 1:
                yield
        gated = jnp.concatenate(normed, axis=-1) * nw_ref[...] * (z * jax.nn.sigmoid(z))
        yield
        mix = _dot(gated.astype(BF16), wo_ref[...])
        return DEEPNORM_ALPHA * x_ref[rows, :] + mix

    _layer_tail(mixer, p_ref, ln_ref, w1_ref, w2_ref, gw_ref, gb_ref, pp_ref, out_ref)


def _gdn_layer(o, z, x, p, layer, nw, wo, tail_w):
    B, S, D = x.shape
    ple = p.shape[-1]
    tm = ROW_TILE
    row_spec = pl.BlockSpec((None, tm, D), lambda b, i: (b, i, 0))
    in_specs = [
        row_spec, row_spec, row_spec,
        pl.BlockSpec((None, None, tm, ple), lambda b, i: (layer, b, i, 0)),
        _const_spec(nw.shape), _layer_spec(wo.shape[1:], 0),
    ] + _tail_specs(layer, D, tail_w[1].shape[-1], ple)
    return pl.pallas_call(
        _gdn_layer_kernel,
        grid=(B, S // tm),
        in_specs=in_specs,
        out_specs=row_spec,
        out_shape=jax.ShapeDtypeStruct((B, S, D), F32),
        compiler_params=pltpu.CompilerParams(
            dimension_semantics=("arbitrary", "arbitrary"), vmem_limit_bytes=VMEM_LIMIT),
        name="gdn_layer",
    )(o, z, x, p, nw, wo, *tail_w)


def _tail_weights(ln_gain, ln_bias, mlp_w1, mlp_w2, ple_gate_w, ple_gate_b, ple_proj):
    ln = jnp.stack([ln_gain[:, 0], ln_bias[:, 0], ln_gain[:, 1], ln_bias[:, 1]], axis=1)
    return (ln, mlp_w1.astype(BF16), mlp_w2.astype(BF16), ple_gate_w.astype(BF16),
            ple_gate_b[:, None, :], ple_proj.astype(BF16))


def _pad_lanes(vec, offset):
    return jnp.zeros((1, LANES), F32).at[0, offset:offset + vec.shape[0]].set(vec)


def kernel(x, p, ln_gain, ln_bias, pool_w, pool_b, pool_scale, gdn_w_in, gdn_conv, gdn_a_log, gdn_dt_bias,
           gdn_norm_w, gdn_w_out, mlp_w1, mlp_w2, ple_gate_w, ple_gate_b, ple_proj):
    B, S, D = x.shape
    assert S % ROW_TILE == 0 and S % DELTA_TILE == 0 and ROW_TILE % CHUNK == 0
    assert DELTA_TILE % (CHUNK * DELTA_CHUNKS) == 0 and ROW_TILE % SUB_TILE == 0
    assert D == GDN_HEADS * HEAD_DIM and D % len(POOL_WINDOWS) == 0
    tail_w = _tail_weights(ln_gain, ln_bias, mlp_w1, mlp_w2, ple_gate_w, ple_gate_b, ple_proj)

    x = _pool_layer(x, p, 0, pool_w.astype(BF16), pool_b, pool_scale[0][None, :], tail_w)

    wba = jnp.pad(gdn_w_in[0][:, 4 * D:], ((0, 0), (0, LANES - 2 * GDN_HEADS))).astype(BF16)
    alog = _pad_lanes(gdn_a_log[0], GDN_HEADS)
    dtb = _pad_lanes(gdn_dt_bias[0], GDN_HEADS)
    q, k, v, z, beta, gc = _gdn_proj(x, gdn_w_in.astype(BF16), wba, gdn_conv, alog, dtb)
    o = _gdn_delta(q, k, v, beta, gc)
    nw = jnp.tile(gdn_norm_w[0], GDN_HEADS)[None, :]
    return _gdn_layer(o, z, x, p, 1, nw, gdn_w_out.astype(BF16), tail_w)
```

```python
import functools

import jax
import jax.numpy as jnp
from jax import lax
from jax.experimental import pallas as pl
from jax.experimental.pallas import tpu as pltpu

F32 = jnp.float32
BF16 = jnp.bfloat16

DEPTH = 2
POOL_WINDOWS = (2, 4, 8, 16)
POOL_HALO = 16
GDN_HEADS = 8
HEAD_DIM = 128
CONV_WIDTH = 4
CONV_HALO = 8
CHUNK = 64
DEEPNORM_ALPHA = (2.0 * DEPTH) ** 0.25
LN_EPS = 1e-5
RMS_EPS = 1e-6
L2_EPS = 1e-6

LANES = 128
ROW_TILE = 512
TAIL_TILE = 1024
SUB_TILE = 256
DELTA_TILE = 256
DELTA_CHUNKS = 4
FF_CHUNK = 512
PROJ_COLS = 256
PROJ_ROWS = 256
VMEM_LIMIT = 56 * 1024 * 1024


def _dot(a, b):
    return jnp.dot(a, b, preferred_element_type=F32)


def _dot_nt(a, b):
    return lax.dot_general(a, b, (((1,), (1,)), ((), ())), preferred_element_type=F32)


def _split_bf16(x):
    hi = x.astype(BF16)
    lo = (x - hi.astype(F32)).astype(BF16)
    return hi, lo


def _dot_split(x, w_bf16):
    hi, lo = _split_bf16(x)
    return _dot(hi, w_bf16) + _dot(lo, w_bf16)


def _layer_norm(y, g, b):
    mu = jnp.mean(y, axis=-1, keepdims=True)
    d = y - mu
    var = jnp.mean(d * d, axis=-1, keepdims=True)
    return d * lax.rsqrt(var + LN_EPS) * g + b


class _Staged:
    def __init__(self, stages):
        self._stages = stages
        self._done = False
        self._value = None

    def step(self):
        if not self._done:
            try:
                next(self._stages)
            except StopIteration as stop:
                self._done, self._value = True, stop.value

    def finish(self):
        while not self._done:
            self.step()
        return self._value


def _layer_tail(mixer, p_ref, ln_ref, w1_ref, w2_ref, gw_ref, gb_ref, pp_ref, o_ref):
    n_sub = o_ref.shape[0] // SUB_TILE
    n_chunks = w1_ref.shape[1] // FF_CHUNK

    def first_norm(y):
        x1 = _layer_norm(y, ln_ref[0:1, :], ln_ref[1:2, :])
        return x1, x1.astype(BF16)

    def hidden(x1b, c):
        return _dot(x1b, w1_ref[:, c * FF_CHUNK:(c + 1) * FF_CHUNK])

    sub_rows = [slice(s * SUB_TILE, (s + 1) * SUB_TILE) for s in range(n_sub)]
    x1, x1b = first_norm(_Staged(mixer(sub_rows[0])).finish())
    for s, rows in enumerate(sub_rows):
        upcoming = _Staged(mixer(sub_rows[s + 1])) if s + 1 < n_sub else None
        acc = None
        h_next = hidden(x1b, 0)
        for c in range(n_chunks):
            h = h_next
            if c + 1 < n_chunks:
                h_next = hidden(x1b, c + 1)
            hb = jnp.square(jnp.maximum(h, 0.0)).astype(BF16)
            t = _dot(hb, w2_ref[c * FF_CHUNK:(c + 1) * FF_CHUNK, :])
            acc = t if acc is None else acc + t
            if upcoming is not None:
                if c < n_chunks - 2:
                    upcoming.step()
                elif c == n_chunks - 2:
                    x1_up, x1b_up = first_norm(upcoming.finish())
        gate = jax.nn.sigmoid(_dot(x1b, gw_ref[...]) + gb_ref[...])
        pe = gate * _dot(p_ref[rows, :].astype(BF16), pp_ref[...])
        y2 = DEEPNORM_ALPHA * x1 + acc + pe
        o_ref[rows, :] = _layer_norm(y2, ln_ref[2:3, :], ln_ref[3:4, :])
        if upcoming is not None:
            x1, x1b = x1_up, x1b_up


def _pool_layer_kernel(x_ref, halo_ref, p_ref, pw_ref, pb_ref, ps_ref, ln_ref, w1_ref, w2_ref,
                       gw_ref, gb_ref, pp_ref, o_ref, xe_ref):
    i = pl.program_id(1)
    tm = x_ref.shape[0]
    group = pw_ref.shape[1]
    xe_ref[0:POOL_HALO, :] = jnp.where(i == 0, 0.0, halo_ref[...])
    xe_ref[POOL_HALO:, :] = x_ref[...]

    def mixer(rows):
        n = rows.stop - rows.start
        pos = (lax.broadcasted_iota(jnp.int32, (n, 1), 0) + (i * tm + rows.start + 1)).astype(F32)
        x = x_ref[rows, :]
        mixes = []
        for gi, w in enumerate(POOL_WINDOWS):
            sl = slice(gi * group, (gi + 1) * group)
            s = xe_ref[rows.start:rows.stop + POOL_HALO, sl]
            m = 1
            while m < w:
                s = s + pltpu.roll(s, m, 0)
                m *= 2
            mean = s[POOL_HALO:] * (1.0 / jnp.minimum(pos, float(w)))
            pooled = (mean - x[:, sl]).astype(BF16)
            mixes.append(_dot(pooled, pw_ref[gi]) + pb_ref[gi:gi + 1, :])
            yield
        mix = jnp.concatenate(mixes, axis=-1) * ps_ref[...]
        return DEEPNORM_ALPHA * x + mix

    _layer_tail(mixer, p_ref, ln_ref, w1_ref, w2_ref, gw_ref, gb_ref, pp_ref, o_ref)


def _fixed_spec(block_shape, index):
    return pl.BlockSpec(block_shape, lambda b, i: index, pipeline_mode=pl.Buffered(1))


def _const_spec(shape):
    return _fixed_spec(shape, (0,) * len(shape))


def _layer_spec(shape, layer):
    return _fixed_spec((None,) + tuple(shape), (layer,) + (0,) * len(shape))


def _tail_specs(layer, d, d_ff, ple):
    shapes = [(4, d), (d, d_ff), (d_ff, d), (d, d), (1, d), (ple, d)]
    return [_layer_spec(s, layer) for s in shapes]


def _pool_layer(x, p, layer, pw, pb, ps, tail_w):
    B, S, D = x.shape
    ple = p.shape[-1]
    tm = TAIL_TILE
    halo_blocks = tm // POOL_HALO
    row_spec = pl.BlockSpec((None, tm, D), lambda b, i: (b, i, 0))
    in_specs = [
        row_spec,
        pl.BlockSpec((None, POOL_HALO, D), lambda b, i: (b, jnp.maximum(i * halo_blocks - 1, 0), 0)),
        pl.BlockSpec((None, None, tm, ple), lambda b, i: (layer, b, i, 0)),
        _layer_spec(pw.shape[1:], 0), _layer_spec(pb.shape[1:], 0), _const_spec(ps.shape),
    ] + _tail_specs(layer, D, tail_w[1].shape[-1], ple)
    return pl.pallas_call(
        _pool_layer_kernel,
        grid=(B, S // tm),
        in_specs=in_specs,
        out_specs=row_spec,
        out_shape=jax.ShapeDtypeStruct((B, S, D), F32),
        scratch_shapes=[pltpu.VMEM((tm + POOL_HALO, D), F32)],
        compiler_params=pltpu.CompilerParams(
            dimension_semantics=("arbitrary", "arbitrary"), vmem_limit_bytes=VMEM_LIMIT),
        name="pool_layer",
    )(x, x, p, pw, pb, ps, *tail_w)


def _softplus(x):
    return jnp.maximum(x, 0.0) + jnp.log1p(jnp.exp(-jnp.abs(x)))


def _gdn_proj_kernel(x_ref, wqkv_ref, wz_ref, cw_ref, wba_ref, alog_ref, dtb_ref, tri_ref, eb_ref, ea_ref,
                     q_ref, k_ref, v_ref, z_ref, beta_ref, gc_ref, carry_ref):
    i = pl.program_id(1)
    tm, d = x_ref.shape

    @pl.when(i == 0)
    def _():
        carry_ref[...] = jnp.zeros_like(carry_ref)

    xb = x_ref[...].astype(BF16)
    n_col = d // PROJ_COLS
    col_chunks = [slice(c * PROJ_COLS, (c + 1) * PROJ_COLS) for c in range(n_col)]

    row_blocks = [slice(r, r + PROJ_ROWS) for r in range(0, tm, PROJ_ROWS)]

    def project(part, cc, rb):
        return _dot(xb[row_blocks[rb], :], wqkv_ref[:, part * d + cc * PROJ_COLS:part * d + (cc + 1) * PROJ_COLS])

    def conv_act(part, cc, pre):
        lc = col_chunks[cc]
        cs = slice(part * d + lc.start, part * d + lc.stop)
        ext = jnp.concatenate([carry_ref[part, :, lc], pre], axis=0)
        carry_ref[part, :, lc] = pre[PROJ_ROWS - CONV_HALO:, :]
        conv = pre * cw_ref[CONV_WIDTH - 1:CONV_WIDTH, cs]
        for j in range(CONV_WIDTH - 1):
            back = CONV_WIDTH - 1 - j
            conv = conv + pltpu.roll(ext, back, 0)[CONV_HALO:] * cw_ref[j:j + 1, cs]
        return conv * jax.nn.sigmoid(conv)

    def store_normalized(out_ref, cc, rb, act, scale):
        for h in range(PROJ_COLS // HEAD_DIM):
            a = act[:, h * HEAD_DIM:(h + 1) * HEAD_DIM]
            r = lax.rsqrt(jnp.sum(a * a, axis=-1, keepdims=True) + L2_EPS)
            col = col_chunks[cc].start + h * HEAD_DIM
            out_ref[row_blocks[rb], col:col + HEAD_DIM] = (a * (r * scale)).astype(out_ref.dtype)

    def gates():
        ba = _dot(xb, wba_ref[...])
        beta = jax.nn.sigmoid(ba).astype(beta_ref.dtype)
        g_hi, g_lo = _split_bf16(-jnp.exp(alog_ref[...]) * _softplus(ba + dtb_ref[...]))
        yield
        gc = _dot(tri_ref[...], g_hi) + _dot(tri_ref[...], g_lo)
        gc_hi, gc_lo = _split_bf16(gc)
        yield
        for lc in col_chunks:
            for rows in row_blocks:
                z_ref[rows, lc] = _dot(xb[rows, :], wz_ref[:, lc]).astype(z_ref.dtype)
                yield
                beta_ref[rows, lc] = _dot(beta[rows, :], eb_ref[:, lc]).astype(beta_ref.dtype)
                gc_ref[rows, lc] = _dot(gc_hi[rows, :], ea_ref[:, lc]) + _dot(gc_lo[rows, :], ea_ref[:, lc])
                yield

    gate_stages = _Staged(gates())
    jobs = [(part, cc, rb) for part in range(3) for cc in range(n_col) for rb in range(len(row_blocks))]
    pre_next = project(*jobs[0])
    for idx, (part, cc, rb) in enumerate(jobs):
        pre = pre_next
        if idx + 1 < len(jobs):
            pre_next = project(*jobs[idx + 1])
        gate_stages.step()
        act = conv_act(part, cc, pre)
        if part == 0:
            store_normalized(q_ref, cc, rb, act, HEAD_DIM ** -0.5)
        elif part == 1:
            store_normalized(k_ref, cc, rb, act, 1.0)
        else:
            v_ref[row_blocks[rb], col_chunks[cc]] = act.astype(v_ref.dtype)
    gate_stages.finish()


def _gdn_proj(x, w_in, wba, cw, alog, dtb):
    B, S, D = x.shape
    tm = ROW_TILE
    r = jnp.arange(tm)
    tri = ((r[:, None] >= r[None, :]) & (r[:, None] // CHUNK == r[None, :] // CHUNK)).astype(BF16)
    lane_head = jnp.arange(GDN_HEADS * HEAD_DIM) // HEAD_DIM
    src = jnp.arange(LANES)
    eb = (src[:, None] == lane_head[None, :]).astype(BF16)
    ea = (src[:, None] == lane_head[None, :] + GDN_HEADS).astype(BF16)
    row_spec = pl.BlockSpec((None, tm, D), lambda b, i: (b, i, 0))
    consts = (wba, alog, dtb, tri, eb, ea)
    in_specs = [row_spec,
                _fixed_spec((None, D, 3 * D), (0, 0, 0)),
                _fixed_spec((None, D, D), (0, 0, 3)),
                _layer_spec(cw.shape[1:], 0)] + [_const_spec(c.shape) for c in consts]
    out_dtypes = [BF16, BF16, BF16, BF16, BF16, F32]
    return pl.pallas_call(
        _gdn_proj_kernel,
        grid=(B, S // tm),
        in_specs=in_specs,
        out_specs=[row_spec] * 6,
        out_shape=[jax.ShapeDtypeStruct((B, S, D), dt) for dt in out_dtypes],
        scratch_shapes=[pltpu.VMEM((3, CONV_HALO, D), F32)],
        compiler_params=pltpu.CompilerParams(
            dimension_semantics=("arbitrary", "arbitrary"), vmem_limit_bytes=VMEM_LIMIT),
        name="gdn_proj",
    )(x, w_in, w_in, cw, *consts)


def _gdn_delta_kernel(q_ref, k_ref, v_ref, beta_ref, gc_ref, o_ref, state_ref):
    i = pl.program_id(0)
    nb, ts, _ = q_ref.shape
    c = CHUNK
    dh = HEAD_DIM
    assert 2 * c == dh
    half = GDN_HEADS // 2
    n_pairs = nb * half
    pairs = range(DELTA_CHUNKS * n_pairs)

    @pl.when(i == 0)
    def _():
        state_ref[...] = jnp.zeros_like(state_ref)

    ii = lax.broadcasted_iota(jnp.int32, (c, dh), 0)
    lane = lax.broadcasted_iota(jnp.int32, (c, dh), 1)
    left = lane < c
    jj = jnp.where(left, lane, lane - c)
    causal = ii >= jj
    strict = ii > jj
    eye = jnp.where(ii == jj, 1.0, 0.0)
    left_sq = lax.broadcasted_iota(jnp.int32, (dh, dh), 1) < c
    head_a = lax.broadcasted_iota(jnp.int32, (c, 2 * dh), 1) < dh

    def block_diag(x):
        return jnp.concatenate([jnp.where(left, x, 0.0), jnp.where(left, 0.0, x)], axis=0)

    def stack_heads(x):
        return jnp.concatenate([x[:, :dh], x[:, dh:]], axis=0)

    def chunk_body(ci, carry):
        rows = [pl.ds(pl.multiple_of((ci * DELTA_CHUNKS + j) * c, c), c) for j in range(DELTA_CHUNKS)]
        at = [(p % n_pairs // half, rows[p // n_pairs], slice(p % half * 2 * dh, (p % half + 1) * 2 * dh))
              for p in pairs]
        gc = [gc_ref[at[p]] for p in pairs]
        beta = [beta_ref[at[p]].astype(F32) for p in pairs]
        k = [k_ref[at[p]].astype(F32) for p in pairs]
        kb = [k[p] * beta[p] for p in pairs]
        decay = []
        for p in pairs:
            gi = jnp.where(left, gc[p][:, :dh], gc[p][:, dh:])
            gj = stack_heads(gc[p]).T[:c]
            decay.append(jnp.exp(jnp.where(causal, gi - gj, -jnp.inf)))
        akq = []
        for p in pairs:
            kp = k_ref[at[p]]
            k_diag = jnp.concatenate([jnp.where(head_a, kp, 0), jnp.where(head_a, 0, kp)], axis=0)
            lhs = jnp.concatenate([kb[p].astype(BF16), q_ref[at[p]]], axis=0)
            akq.append(_dot_nt(lhs, k_diag))
        n = [jnp.where(strict, akq[p][:c] * decay[p], 0.0) for p in pairs]
        qk = [akq[p][c:] * decay[p] for p in pairs]

        t = [eye - n[p] for p in pairs]
        pw = [_dot(n[p].astype(BF16), block_diag(n[p]).astype(BF16)) for p in pairs]
        terms = 2
        while terms < c:
            pw_diag = [block_diag(pw[p]).astype(BF16) for p in pairs]
            if 2 * terms < c:
                both = [_dot(jnp.concatenate([pw[p], t[p]], axis=0).astype(BF16), pw_diag[p]) for p in pairs]
                pw = [both[p][:c] for p in pairs]
                t = [t[p] + both[p][c:] for p in pairs]
            else:
                t = [t[p] + _dot(t[p].astype(BF16), pw_diag[p]) for p in pairs]
            terms *= 2

        eg = [jnp.exp(gc[p]) for p in pairs]
        uw = []
        for p in pairs:
            vb = v_ref[at[p]].astype(F32) * beta[p]
            kbg = kb[p] * eg[p]
            rhs = jnp.concatenate([jnp.concatenate([vb[:, :dh], kbg[:, :dh]], axis=1),
                                   jnp.concatenate([vb[:, dh:], kbg[:, dh:]], axis=1)], axis=0)
            uw.append(_dot(block_diag(t[p]).astype(BF16), rhs.astype(BF16)).astype(BF16))
        f = {}

        def f_stage(j):
            for p in range(j * n_pairs, (j + 1) * n_pairs):
                kd = k[p] * jnp.exp(gc[p][c - 1:c, :] - gc[p])
                kd_t = stack_heads(kd).T
                lhs = jnp.concatenate(
                    [block_diag(qk[p]), jnp.where(left_sq, kd_t, 0.0), jnp.where(left_sq, 0.0, kd_t)], axis=0)
                f[p] = _dot(lhs.astype(BF16), uw[p])

        def f_parts(p, s):
            return f[p][s * c:(s + 1) * c], f[p][2 * c + s * dh:2 * c + (s + 1) * dh]

        f_stage(0)
        for j in range(DELTA_CHUNKS):
            r = {}
            for p in range(j * n_pairs, (j + 1) * n_pairs):
                qe = q_ref[at[p]].astype(F32) * eg[p]
                for s in range(2):
                    h = 2 * (p % n_pairs) + s
                    f_qk, f_kd = f_parts(p, s)
                    m = jnp.concatenate([-f_kd[:, dh:], qe[:, s * dh:(s + 1) * dh] - f_qk[:, dh:]], axis=0)
                    r[h] = _dot(m.astype(BF16), state_ref[h].astype(BF16))
            if j + 1 < DELTA_CHUNKS:
                f_stage(j + 1)
            for p in range(j * n_pairs, (j + 1) * n_pairs):
                b, chunk_rows, cols = at[p]
                for s in range(2):
                    h = 2 * (p % n_pairs) + s
                    f_qk, f_kd = f_parts(p, s)
                    decay_last = eg[p][c - 1:c, s * dh:(s + 1) * dh]
                    state_ref[h] = decay_last * state_ref[h] + r[h][:dh] + f_kd[:, :dh]
                    o_ref[b, chunk_rows, cols.start + s * dh:cols.start + (s + 1) * dh] = (
                        r[h][dh:] + f_qk[:, :dh]).astype(o_ref.dtype)
        return carry

    lax.fori_loop(0, ts // (c * DELTA_CHUNKS), chunk_body, 0)


def _gdn_delta(q, k, v, beta, gc):
    B, S, D = q.shape
    ts = DELTA_TILE
    row_spec = pl.BlockSpec((B, ts, D), lambda i: (0, i, 0))
    return pl.pallas_call(
        _gdn_delta_kernel,
        grid=(S // ts,),
        in_specs=[row_spec] * 5,
        out_specs=row_spec,
        out_shape=jax.ShapeDtypeStruct((B, S, D), BF16),
        scratch_shapes=[pltpu.VMEM((B * GDN_HEADS, HEAD_DIM, HEAD_DIM), F32)],
        compiler_params=pltpu.CompilerParams(dimension_semantics=("arbitrary",), vmem_limit_bytes=VMEM_LIMIT),
        name="gdn_delta",
    )(q, k, v, beta, gc)


def _gdn_layer_kernel(o_ref, z_ref, x_ref, p_ref, nw_ref, wo_ref, ln_ref, w1_ref, w2_ref,
                      gw_ref, gb_ref, pp_ref, out_ref):
    def mixer(rows):
        z = z_ref[rows, :].astype(F32)
        normed = []
        for h in range(GDN_HEADS):
            oh = o_ref[rows, h * HEAD_DIM:(h + 1) * HEAD_DIM].astype(F32)
            normed.append(oh * lax.rsqrt(jnp.mean(oh * oh, axis=-1, keepdims=True) + RMS_EPS))
            if h % 2 == 1:
                yield
        gated = jnp.concatenate(normed, axis=-1) * nw_ref[...] * (z * jax.nn.sigmoid(z))
        yield
        mix = _dot(gated.astype(BF16), wo_ref[...])
        return DEEPNORM_ALPHA * x_ref[rows, :] + mix

    _layer_tail(mixer, p_ref, ln_ref, w1_ref, w2_ref, gw_ref, gb_ref, pp_ref, out_ref)


def _gdn_layer(o, z, x, p, layer, nw, wo, tail_w):
    B, S, D = x.shape
    ple = p.shape[-1]
    tm = TAIL_TILE
    row_spec = pl.BlockSpec((None, tm, D), lambda b, i: (b, i, 0))
    in_specs = [
        row_spec, row_spec, row_spec,
        pl.BlockSpec((None, None, tm, ple), lambda b, i: (layer, b, i, 0)),
        _const_spec(nw.shape), _layer_spec(wo.shape[1:], 0),
    ] + _tail_specs(layer, D, tail_w[1].shape[-1], ple)
    return pl.pallas_call(
        _gdn_layer_kernel,
        grid=(B, S // tm),
        in_specs=in_specs,
        out_specs=row_spec,
        out_shape=jax.ShapeDtypeStruct((B, S, D), F32),
        compiler_params=pltpu.CompilerParams(
            dimension_semantics=("arbitrary", "arbitrary"), vmem_limit_bytes=VMEM_LIMIT),
        name="gdn_layer",
    )(o, z, x, p, nw, wo, *tail_w)


def _tail_weights(ln_gain, ln_bias, mlp_w1, mlp_w2, ple_gate_w, ple_gate_b, ple_proj):
    ln = jnp.stack([ln_gain[:, 0], ln_bias[:, 0], ln_gain[:, 1], ln_bias[:, 1]], axis=1)
    return (ln, mlp_w1.astype(BF16), mlp_w2.astype(BF16), ple_gate_w.astype(BF16),
            ple_gate_b[:, None, :], ple_proj.astype(BF16))


def _pad_lanes(vec, offset):
    return jnp.zeros((1, LANES), F32).at[0, offset:offset + vec.shape[0]].set(vec)


def kernel(x, p, ln_gain, ln_bias, pool_w, pool_b, pool_scale, gdn_w_in, gdn_conv, gdn_a_log, gdn_dt_bias,
           gdn_norm_w, gdn_w_out, mlp_w1, mlp_w2, ple_gate_w, ple_gate_b, ple_proj):
    B, S, D = x.shape
    assert S % ROW_TILE == 0 and S % TAIL_TILE == 0 and S % DELTA_TILE == 0 and ROW_TILE % CHUNK == 0
    assert DELTA_TILE % (CHUNK * DELTA_CHUNKS) == 0 and TAIL_TILE % SUB_TILE == 0 and ROW_TILE % PROJ_ROWS == 0
    assert D == GDN_HEADS * HEAD_DIM and D % len(POOL_WINDOWS) == 0
    tail_w = _tail_weights(ln_gain, ln_bias, mlp_w1, mlp_w2, ple_gate_w, ple_gate_b, ple_proj)

    x = _pool_layer(x, p, 0, pool_w.astype(BF16), pool_b, pool_scale[0][None, :], tail_w)

    wba = jnp.pad(gdn_w_in[0][:, 4 * D:], ((0, 0), (0, LANES - 2 * GDN_HEADS))).astype(BF16)
    alog = _pad_lanes(gdn_a_log[0], GDN_HEADS)
    dtb = _pad_lanes(gdn_dt_bias[0], GDN_HEADS)
    q, k, v, z, beta, gc = _gdn_proj(x, gdn_w_in.astype(BF16), wba, gdn_conv, alog, dtb)
    o = _gdn_delta(q, k, v, beta, gc)
    nw = jnp.tile(gdn_norm_w[0], GDN_HEADS)[None, :]
    return _gdn_layer(o, z, x, p, 1, nw, gdn_w_out.astype(BF16), tail_w)
```

```python
import functools

import jax
import jax.numpy as jnp
from jax import lax
from jax.experimental import pallas as pl
from jax.experimental.pallas import tpu as pltpu

F32 = jnp.float32
BF16 = jnp.bfloat16

DEPTH = 2
POOL_WINDOWS = (2, 4, 8, 16)
POOL_HALO = 16
GDN_HEADS = 8
HEAD_DIM = 128
CONV_WIDTH = 4
CONV_HALO = 8
CHUNK = 64
DEEPNORM_ALPHA = (2.0 * DEPTH) ** 0.25
LN_EPS = 1e-5
RMS_EPS = 1e-6
L2_EPS = 1e-6

LANES = 128
ROW_TILE = 512
TAIL_TILE = 1024
SUB_TILE = 256
DELTA_TILE = 256
DELTA_CHUNKS = 4
FF_CHUNK = 512
PROJ_COLS = 256
PROJ_ROWS = 256
VMEM_LIMIT = 56 * 1024 * 1024


def _dot(a, b):
    return jnp.dot(a, b, preferred_element_type=F32)


def _dot_nt(a, b):
    return lax.dot_general(a, b, (((1,), (1,)), ((), ())), preferred_element_type=F32)


def _split_bf16(x):
    hi = x.astype(BF16)
    lo = (x - hi.astype(F32)).astype(BF16)
    return hi, lo


def _dot_split(x, w_bf16):
    hi, lo = _split_bf16(x)
    return _dot(hi, w_bf16) + _dot(lo, w_bf16)


def _layer_norm(y, g, b):
    mu = jnp.mean(y, axis=-1, keepdims=True)
    d = y - mu
    var = jnp.mean(d * d, axis=-1, keepdims=True)
    return d * lax.rsqrt(var + LN_EPS) * g + b


class _Staged:
    def __init__(self, stages):
        self._stages = stages
        self._done = False
        self._value = None

    def step(self):
        if not self._done:
            try:
                next(self._stages)
            except StopIteration as stop:
                self._done, self._value = True, stop.value

    def finish(self):
        while not self._done:
            self.step()
        return self._value


def _layer_tail(mixer, p_ref, ln_ref, w1_ref, w2_ref, gw_ref, gb_ref, pp_ref, o_ref):
    n_sub = o_ref.shape[0] // SUB_TILE
    n_chunks = w1_ref.shape[1] // FF_CHUNK

    def first_norm(y):
        x1 = _layer_norm(y, ln_ref[0:1, :], ln_ref[1:2, :])
        return x1, x1.astype(BF16)

    def hidden(x1b, c):
        return _dot(x1b, w1_ref[:, c * FF_CHUNK:(c + 1) * FF_CHUNK])

    sub_rows = [slice(s * SUB_TILE, (s + 1) * SUB_TILE) for s in range(n_sub)]
    x1, x1b = first_norm(_Staged(mixer(sub_rows[0])).finish())
    for s, rows in enumerate(sub_rows):
        upcoming = _Staged(mixer(sub_rows[s + 1])) if s + 1 < n_sub else None
        acc = None
        h_next = hidden(x1b, 0)
        for c in range(n_chunks):
            h = h_next
            if c + 1 < n_chunks:
                h_next = hidden(x1b, c + 1)
            hb = jnp.square(jnp.maximum(h, 0.0)).astype(BF16)
            t = _dot(hb, w2_ref[c * FF_CHUNK:(c + 1) * FF_CHUNK, :])
            acc = t if acc is None else acc + t
            if upcoming is not None:
                if c < n_chunks - 2:
                    upcoming.step()
                elif c == n_chunks - 2:
                    x1_up, x1b_up = first_norm(upcoming.finish())
        gate = jax.nn.sigmoid(_dot(x1b, gw_ref[...]) + gb_ref[...])
        pe = gate * _dot(p_ref[rows, :].astype(BF16), pp_ref[...])
        y2 = DEEPNORM_ALPHA * x1 + acc + pe
        o_ref[rows, :] = _layer_norm(y2, ln_ref[2:3, :], ln_ref[3:4, :])
        if upcoming is not None:
            x1, x1b = x1_up, x1b_up


def _pool_layer_kernel(x_ref, halo_ref, p_ref, pw_ref, pb_ref, ps_ref, ln_ref, w1_ref, w2_ref,
                       gw_ref, gb_ref, pp_ref, o_ref, xe_ref):
    i = pl.program_id(1)
    tm = x_ref.shape[0]
    group = pw_ref.shape[1]
    xe_ref[0:POOL_HALO, :] = jnp.where(i == 0, 0.0, halo_ref[...])
    xe_ref[POOL_HALO:, :] = x_ref[...]

    def mixer(rows):
        n = rows.stop - rows.start
        pos = (lax.broadcasted_iota(jnp.int32, (n, 1), 0) + (i * tm + rows.start + 1)).astype(F32)
        x = x_ref[rows, :]
        mixes = []
        for gi, w in enumerate(POOL_WINDOWS):
            sl = slice(gi * group, (gi + 1) * group)
            s = xe_ref[rows.start:rows.stop + POOL_HALO, sl]
            m = 1
            while m < w:
                s = s + pltpu.roll(s, m, 0)
                m *= 2
            mean = s[POOL_HALO:] * (1.0 / jnp.minimum(pos, float(w)))
            pooled = (mean - x[:, sl]).astype(BF16)
            mixes.append(_dot(pooled, pw_ref[gi]) + pb_ref[gi:gi + 1, :])
            yield
        mix = jnp.concatenate(mixes, axis=-1) * ps_ref[...]
        return DEEPNORM_ALPHA * x + mix

    _layer_tail(mixer, p_ref, ln_ref, w1_ref, w2_ref, gw_ref, gb_ref, pp_ref, o_ref)


def _fixed_spec(block_shape, index):
    return pl.BlockSpec(block_shape, lambda b, i: index, pipeline_mode=pl.Buffered(1))


def _const_spec(shape):
    return _fixed_spec(shape, (0,) * len(shape))


def _layer_spec(shape, layer):
    return _fixed_spec((None,) + tuple(shape), (layer,) + (0,) * len(shape))


def _tail_specs(layer, d, d_ff, ple):
    shapes = [(4, d), (d, d_ff), (d_ff, d), (d, d), (1, d), (ple, d)]
    return [_layer_spec(s, layer) for s in shapes]


def _pool_layer(x, p, layer, pw, pb, ps, tail_w):
    B, S, D = x.shape
    ple = p.shape[-1]
    tm = TAIL_TILE
    halo_blocks = tm // POOL_HALO
    row_spec = pl.BlockSpec((None, tm, D), lambda b, i: (b, i, 0))
    in_specs = [
        row_spec,
        pl.BlockSpec((None, POOL_HALO, D), lambda b, i: (b, jnp.maximum(i * halo_blocks - 1, 0), 0)),
        pl.BlockSpec((None, None, tm, ple), lambda b, i: (layer, b, i, 0)),
        _layer_spec(pw.shape[1:], 0), _layer_spec(pb.shape[1:], 0), _const_spec(ps.shape),
    ] + _tail_specs(layer, D, tail_w[1].shape[-1], ple)
    return pl.pallas_call(
        _pool_layer_kernel,
        grid=(B, S // tm),
        in_specs=in_specs,
        out_specs=row_spec,
        out_shape=jax.ShapeDtypeStruct((B, S, D), F32),
        scratch_shapes=[pltpu.VMEM((tm + POOL_HALO, D), F32)],
        compiler_params=pltpu.CompilerParams(
            dimension_semantics=("arbitrary", "arbitrary"), vmem_limit_bytes=VMEM_LIMIT),
        name="pool_layer",
    )(x, x, p, pw, pb, ps, *tail_w)


def _softplus(x):
    return jnp.maximum(x, 0.0) + jnp.log1p(jnp.exp(-jnp.abs(x)))


def _gdn_proj_kernel(x_ref, wqkv_ref, wz_ref, cw_ref, wba_ref, alog_ref, dtb_ref, tri_ref, eb_ref, ea_ref,
                     act_ref, gc_ref, carry_ref):
    i = pl.program_id(1)
    tm, d = x_ref.shape
    q_ref, k_ref, v_ref, beta_ref, z_ref = (act_ref.at[:, n * d:(n + 1) * d] for n in range(5))

    @pl.when(i == 0)
    def _():
        carry_ref[...] = jnp.zeros_like(carry_ref)

    xb = x_ref[...].astype(BF16)
    n_col = d // PROJ_COLS
    col_chunks = [slice(c * PROJ_COLS, (c + 1) * PROJ_COLS) for c in range(n_col)]

    row_blocks = [slice(r, r + PROJ_ROWS) for r in range(0, tm, PROJ_ROWS)]

    def project(part, cc, rb):
        return _dot(xb[row_blocks[rb], :], wqkv_ref[:, part * d + cc * PROJ_COLS:part * d + (cc + 1) * PROJ_COLS])

    def conv_act(part, cc, pre):
        lc = col_chunks[cc]
        cs = slice(part * d + lc.start, part * d + lc.stop)
        ext = jnp.concatenate([carry_ref[part, :, lc], pre], axis=0)
        carry_ref[part, :, lc] = pre[PROJ_ROWS - CONV_HALO:, :]
        conv = pre * cw_ref[CONV_WIDTH - 1:CONV_WIDTH, cs]
        for j in range(CONV_WIDTH - 1):
            back = CONV_WIDTH - 1 - j
            conv = conv + pltpu.roll(ext, back, 0)[CONV_HALO:] * cw_ref[j:j + 1, cs]
        return conv * jax.nn.sigmoid(conv)

    def store_normalized(out_ref, cc, rb, act, scale):
        for h in range(PROJ_COLS // HEAD_DIM):
            a = act[:, h * HEAD_DIM:(h + 1) * HEAD_DIM]
            r = lax.rsqrt(jnp.sum(a * a, axis=-1, keepdims=True) + L2_EPS)
            col = col_chunks[cc].start + h * HEAD_DIM
            out_ref[row_blocks[rb], col:col + HEAD_DIM] = (a * (r * scale)).astype(out_ref.dtype)

    def gates():
        ba = _dot(xb, wba_ref[...])
        beta = jax.nn.sigmoid(ba).astype(beta_ref.dtype)
        g_hi, g_lo = _split_bf16(-jnp.exp(alog_ref[...]) * _softplus(ba + dtb_ref[...]))
        yield
        gc = _dot(tri_ref[...], g_hi) + _dot(tri_ref[...], g_lo)
        gc_hi, gc_lo = _split_bf16(gc)
        yield
        for lc in col_chunks:
            for rows in row_blocks:
                z_ref[rows, lc] = _dot(xb[rows, :], wz_ref[:, lc]).astype(z_ref.dtype)
                yield
                beta_ref[rows, lc] = _dot(beta[rows, :], eb_ref[:, lc]).astype(beta_ref.dtype)
                gc_ref[rows, lc] = _dot(gc_hi[rows, :], ea_ref[:, lc]) + _dot(gc_lo[rows, :], ea_ref[:, lc])
                yield

    gate_stages = _Staged(gates())
    jobs = [(part, cc, rb) for part in range(3) for cc in range(n_col) for rb in range(len(row_blocks))]
    pre_next = project(*jobs[0])
    for idx, (part, cc, rb) in enumerate(jobs):
        pre = pre_next
        if idx + 1 < len(jobs):
            pre_next = project(*jobs[idx + 1])
        gate_stages.step()
        act = conv_act(part, cc, pre)
        if part == 0:
            store_normalized(q_ref, cc, rb, act, HEAD_DIM ** -0.5)
        elif part == 1:
            store_normalized(k_ref, cc, rb, act, 1.0)
        else:
            v_ref[row_blocks[rb], col_chunks[cc]] = act.astype(v_ref.dtype)
    gate_stages.finish()


def _gdn_proj(x, w_in, wba, cw, alog, dtb):
    B, S, D = x.shape
    tm = ROW_TILE
    r = jnp.arange(tm)
    tri = ((r[:, None] >= r[None, :]) & (r[:, None] // CHUNK == r[None, :] // CHUNK)).astype(BF16)
    lane_head = jnp.arange(GDN_HEADS * HEAD_DIM) // HEAD_DIM
    src = jnp.arange(LANES)
    eb = (src[:, None] == lane_head[None, :]).astype(BF16)
    ea = (src[:, None] == lane_head[None, :] + GDN_HEADS).astype(BF16)
    row_spec = pl.BlockSpec((None, tm, D), lambda b, i: (b, i, 0))
    consts = (wba, alog, dtb, tri, eb, ea)
    in_specs = [row_spec,
                _fixed_spec((None, D, 3 * D), (0, 0, 0)),
                _fixed_spec((None, D, D), (0, 0, 3)),
                _layer_spec(cw.shape[1:], 0)] + [_const_spec(c.shape) for c in consts]
    return pl.pallas_call(
        _gdn_proj_kernel,
        grid=(B, S // tm),
        in_specs=in_specs,
        out_specs=[pl.BlockSpec((None, tm, 5 * D), lambda b, i: (b, i, 0)), row_spec],
        out_shape=[jax.ShapeDtypeStruct((B, S, 5 * D), BF16), jax.ShapeDtypeStruct((B, S, D), F32)],
        scratch_shapes=[pltpu.VMEM((3, CONV_HALO, D), F32)],
        compiler_params=pltpu.CompilerParams(
            dimension_semantics=("arbitrary", "arbitrary"), vmem_limit_bytes=VMEM_LIMIT),
        name="gdn_proj",
    )(x, w_in, w_in, cw, *consts)


def _gdn_delta_kernel(act_ref, gc_ref, o_ref, state_ref):
    i = pl.program_id(0)
    nb, ts, d = gc_ref.shape
    q_ref, k_ref, v_ref, beta_ref = (act_ref.at[:, :, n * d:(n + 1) * d] for n in range(4))
    c = CHUNK
    dh = HEAD_DIM
    assert 2 * c == dh
    half = GDN_HEADS // 2
    n_pairs = nb * half
    pairs = range(DELTA_CHUNKS * n_pairs)

    @pl.when(i == 0)
    def _():
        state_ref[...] = jnp.zeros_like(state_ref)

    ii = lax.broadcasted_iota(jnp.int32, (c, dh), 0)
    lane = lax.broadcasted_iota(jnp.int32, (c, dh), 1)
    left = lane < c
    jj = jnp.where(left, lane, lane - c)
    causal = ii >= jj
    strict = ii > jj
    eye = jnp.where(ii == jj, 1.0, 0.0)
    left_sq = lax.broadcasted_iota(jnp.int32, (dh, dh), 1) < c
    head_a = lax.broadcasted_iota(jnp.int32, (c, 2 * dh), 1) < dh

    def block_diag(x):
        return jnp.concatenate([jnp.where(left, x, 0.0), jnp.where(left, 0.0, x)], axis=0)

    def stack_heads(x):
        return jnp.concatenate([x[:, :dh], x[:, dh:]], axis=0)

    def chunk_body(ci, carry):
        rows = [pl.ds(pl.multiple_of((ci * DELTA_CHUNKS + j) * c, c), c) for j in range(DELTA_CHUNKS)]
        at = [(p % n_pairs // half, rows[p // n_pairs], slice(p % half * 2 * dh, (p % half + 1) * 2 * dh))
              for p in pairs]
        gc = [gc_ref[at[p]] for p in pairs]
        beta = [beta_ref[at[p]].astype(F32) for p in pairs]
        k = [k_ref[at[p]].astype(F32) for p in pairs]
        kb = [k[p] * beta[p] for p in pairs]
        decay = []
        for p in pairs:
            gi = jnp.where(left, gc[p][:, :dh], gc[p][:, dh:])
            gj = stack_heads(gc[p]).T[:c]
            decay.append(jnp.exp(jnp.where(causal, gi - gj, -jnp.inf)))
        akq = []
        for p in pairs:
            kp = k_ref[at[p]]
            k_diag = jnp.concatenate([jnp.where(head_a, kp, 0), jnp.where(head_a, 0, kp)], axis=0)
            lhs = jnp.concatenate([kb[p].astype(BF16), q_ref[at[p]]], axis=0)
            akq.append(_dot_nt(lhs, k_diag))
        n = [jnp.where(strict, akq[p][:c] * decay[p], 0.0) for p in pairs]
        qk = [akq[p][c:] * decay[p] for p in pairs]

        t = [eye - n[p] for p in pairs]
        pw = [_dot(n[p].astype(BF16), block_diag(n[p]).astype(BF16)) for p in pairs]
        terms = 2
        while terms < c:
            pw_diag = [block_diag(pw[p]).astype(BF16) for p in pairs]
            if 2 * terms < c:
                both = [_dot(jnp.concatenate([pw[p], t[p]], axis=0).astype(BF16), pw_diag[p]) for p in pairs]
                pw = [both[p][:c] for p in pairs]
                t = [t[p] + both[p][c:] for p in pairs]
            else:
                t = [t[p] + _dot(t[p].astype(BF16), pw_diag[p]) for p in pairs]
            terms *= 2

        eg = [jnp.exp(gc[p]) for p in pairs]
        uw = []
        for p in pairs:
            vb = v_ref[at[p]].astype(F32) * beta[p]
            kbg = kb[p] * eg[p]
            rhs = jnp.concatenate([jnp.concatenate([vb[:, :dh], kbg[:, :dh]], axis=1),
                                   jnp.concatenate([vb[:, dh:], kbg[:, dh:]], axis=1)], axis=0)
            uw.append(_dot(block_diag(t[p]).astype(BF16), rhs.astype(BF16)).astype(BF16))
        f = {}

        def f_stage(j):
            for p in range(j * n_pairs, (j + 1) * n_pairs):
                kd = k[p] * jnp.exp(gc[p][c - 1:c, :] - gc[p])
                kd_t = stack_heads(kd).T
                lhs = jnp.concatenate(
                    [block_diag(qk[p]), jnp.where(left_sq, kd_t, 0.0), jnp.where(left_sq, 0.0, kd_t)], axis=0)
                f[p] = _dot(lhs.astype(BF16), uw[p])

        def f_parts(p, s):
            return f[p][s * c:(s + 1) * c], f[p][2 * c + s * dh:2 * c + (s + 1) * dh]

        f_stage(0)
        for j in range(DELTA_CHUNKS):
            r = {}
            for p in range(j * n_pairs, (j + 1) * n_pairs):
                qe = q_ref[at[p]].astype(F32) * eg[p]
                for s in range(2):
                    h = 2 * (p % n_pairs) + s
                    f_qk, f_kd = f_parts(p, s)
                    m = jnp.concatenate([-f_kd[:, dh:], qe[:, s * dh:(s + 1) * dh] - f_qk[:, dh:]], axis=0)
                    r[h] = _dot(m.astype(BF16), state_ref[h].astype(BF16))
            if j + 1 < DELTA_CHUNKS:
                f_stage(j + 1)
            for p in range(j * n_pairs, (j + 1) * n_pairs):
                b, chunk_rows, cols = at[p]
                for s in range(2):
                    h = 2 * (p % n_pairs) + s
                    f_qk, f_kd = f_parts(p, s)
                    decay_last = eg[p][c - 1:c, s * dh:(s + 1) * dh]
                    state_ref[h] = decay_last * state_ref[h] + r[h][:dh] + f_kd[:, :dh]
                    o_ref[b, chunk_rows, cols.start + s * dh:cols.start + (s + 1) * dh] = (
                        r[h][dh:] + f_qk[:, :dh]).astype(o_ref.dtype)
        return carry

    lax.fori_loop(0, ts // (c * DELTA_CHUNKS), chunk_body, 0)


def _gdn_delta(act, gc):
    B, S, D = gc.shape
    ts = DELTA_TILE
    row_spec = pl.BlockSpec((B, ts, D), lambda i: (0, i, 0))
    return pl.pallas_call(
        _gdn_delta_kernel,
        grid=(S // ts,),
        in_specs=[pl.BlockSpec((B, ts, 4 * D), lambda i: (0, i, 0)), row_spec],
        out_specs=row_spec,
        out_shape=jax.ShapeDtypeStruct((B, S, D), BF16),
        scratch_shapes=[pltpu.VMEM((B * GDN_HEADS, HEAD_DIM, HEAD_DIM), F32)],
        compiler_params=pltpu.CompilerParams(dimension_semantics=("arbitrary",), vmem_limit_bytes=VMEM_LIMIT),
        name="gdn_delta",
    )(act, gc)


def _gdn_layer_kernel(o_ref, z_ref, x_ref, p_ref, nw_ref, wo_ref, ln_ref, w1_ref, w2_ref,
                      gw_ref, gb_ref, pp_ref, out_ref):
    def mixer(rows):
        z = z_ref[rows, :].astype(F32)
        normed = []
        for h in range(GDN_HEADS):
            oh = o_ref[rows, h * HEAD_DIM:(h + 1) * HEAD_DIM].astype(F32)
            normed.append(oh * lax.rsqrt(jnp.mean(oh * oh, axis=-1, keepdims=True) + RMS_EPS))
            if h % 2 == 1:
                yield
        gated = jnp.concatenate(normed, axis=-1) * nw_ref[...] * (z * jax.nn.sigmoid(z))
        yield
        mix = _dot(gated.astype(BF16), wo_ref[...])
        return DEEPNORM_ALPHA * x_ref[rows, :] + mix

    _layer_tail(mixer, p_ref, ln_ref, w1_ref, w2_ref, gw_ref, gb_ref, pp_ref, out_ref)


def _gdn_layer(o, act, x, p, layer, nw, wo, tail_w):
    B, S, D = x.shape
    ple = p.shape[-1]
    tm = TAIL_TILE
    row_spec = pl.BlockSpec((None, tm, D), lambda b, i: (b, i, 0))
    in_specs = [
        row_spec, pl.BlockSpec((None, tm, D), lambda b, i: (b, i, 4)), row_spec,
        pl.BlockSpec((None, None, tm, ple), lambda b, i: (layer, b, i, 0)),
        _const_spec(nw.shape), _layer_spec(wo.shape[1:], 0),
    ] + _tail_specs(layer, D, tail_w[1].shape[-1], ple)
    return pl.pallas_call(
        _gdn_layer_kernel,
        grid=(B, S // tm),
        in_specs=in_specs,
        out_specs=row_spec,
        out_shape=jax.ShapeDtypeStruct((B, S, D), F32),
        compiler_params=pltpu.CompilerParams(
            dimension_semantics=("arbitrary", "arbitrary"), vmem_limit_bytes=VMEM_LIMIT),
        name="gdn_layer",
    )(o, act, x, p, nw, wo, *tail_w)


def _tail_weights(ln_gain, ln_bias, mlp_w1, mlp_w2, ple_gate_w, ple_gate_b, ple_proj):
    ln = jnp.stack([ln_gain[:, 0], ln_bias[:, 0], ln_gain[:, 1], ln_bias[:, 1]], axis=1)
    return (ln, mlp_w1.astype(BF16), mlp_w2.astype(BF16), ple_gate_w.astype(BF16),
            ple_gate_b[:, None, :], ple_proj.astype(BF16))


def _pad_lanes(vec, offset):
    return jnp.zeros((1, LANES), F32).at[0, offset:offset + vec.shape[0]].set(vec)


def kernel(x, p, ln_gain, ln_bias, pool_w, pool_b, pool_scale, gdn_w_in, gdn_conv, gdn_a_log, gdn_dt_bias,
           gdn_norm_w, gdn_w_out, mlp_w1, mlp_w2, ple_gate_w, ple_gate_b, ple_proj):
    B, S, D = x.shape
    assert S % ROW_TILE == 0 and S % TAIL_TILE == 0 and S % DELTA_TILE == 0 and ROW_TILE % CHUNK == 0
    assert DELTA_TILE % (CHUNK * DELTA_CHUNKS) == 0 and TAIL_TILE % SUB_TILE == 0 and ROW_TILE % PROJ_ROWS == 0
    assert D == GDN_HEADS * HEAD_DIM and D % len(POOL_WINDOWS) == 0
    tail_w = _tail_weights(ln_gain, ln_bias, mlp_w1, mlp_w2, ple_gate_w, ple_gate_b, ple_proj)

    x = _pool_layer(x, p, 0, pool_w.astype(BF16), pool_b, pool_scale[0][None, :], tail_w)

    wba = jnp.pad(gdn_w_in[0][:, 4 * D:], ((0, 0), (0, LANES - 2 * GDN_HEADS))).astype(BF16)
    alog = _pad_lanes(gdn_a_log[0], GDN_HEADS)
    dtb = _pad_lanes(gdn_dt_bias[0], GDN_HEADS)
    act, gc = _gdn_proj(x, gdn_w_in.astype(BF16), wba, gdn_conv, alog, dtb)
    o = _gdn_delta(act, gc)
    nw = jnp.tile(gdn_norm_w[0], GDN_HEADS)[None, :]
    return _gdn_layer(o, act, x, p, 1, nw, gdn_w_out.astype(BF16), tail_w)
```
